```python
import math
import jax, jax.numpy as jnp
from jax import lax
import numpy as np

D_MODEL = 1024
BATCH = 8
SEQ = 2048
DEPTH = 2
DEC_BATCH = 128
DEC_SEQ = 4
PAST_LEN = 16384
PAGE_SIZE = 128

HEAD_DIM = 64
N_HEADS = D_MODEL // HEAD_DIM
N_KV_HEADS_A = 4
WINDOW_A = 128
DILATED_GROUPS = ((128, 1), (512, 4), (2048, 16))
N_GROUPS_B = len(DILATED_GROUPS)
BLOCK = 128
NUM_BUCKETS = 32
MAX_DISTANCE = 2048
D_FF = ((8 * D_MODEL // 3 + 127) // 128) * 128
ALPHA = (2 * DEPTH) ** 0.25
BETA = (8 * DEPTH) ** -0.25
LN_EPS = 1e-5
NEG_INF = -1e30
N_LAYERS_A = (DEPTH + 1) // 2
N_LAYERS_B = DEPTH // 2
QKV_A = (N_HEADS + 2 * N_KV_HEADS_A) * HEAD_DIM
QKV_B = N_GROUPS_B * 3 * N_HEADS * HEAD_DIM
SCALE = HEAD_DIM ** -0.5

kernel_name = "hybrid_swa_sink_dilated_macaron_deepnorm_step"


def t5_bucket(dist):
    max_exact = NUM_BUCKETS // 2
    d = jnp.maximum(dist.astype(jnp.float32), 1.0)
    large = max_exact + (jnp.log(d / max_exact) / math.log(MAX_DISTANCE / max_exact)
                         * (NUM_BUCKETS - max_exact)).astype(jnp.int32)
    large = jnp.minimum(large, NUM_BUCKETS - 1)
    return jnp.where(dist < max_exact, dist, large)


def rel_bias(table, dist):
    return jnp.moveaxis(table.astype(jnp.float32)[t5_bucket(dist)], -1, 0)


def layer_norm(x, g, b):
    xf = x.astype(jnp.float32)
    mu = jnp.mean(xf, -1, keepdims=True)
    var = jnp.mean(jnp.square(xf - mu), -1, keepdims=True)
    y = (xf - mu) * lax.rsqrt(var + LN_EPS) * g.astype(jnp.float32) + b.astype(jnp.float32)
    return y.astype(x.dtype)


def swiglu(x, w_gate, w_up, w_down):
    return (jax.nn.silu(x @ w_gate) * (x @ w_up)) @ w_down


def masked_softmax(s, valid):
    s = jnp.where(valid, s, NEG_INF)
    m = jnp.max(s, -1, keepdims=True)
    e = jnp.exp(s - m)
    den = jnp.sum(e, -1, keepdims=True)
    return e / den, (m + jnp.log(den))[..., 0]


def banded_attention(q, k, v, n_invalid, window_units, dilation, table):
    n, L, hq, dh = q.shape
    hkv = k.shape[2]
    g = hq // hkv
    nb = L // BLOCK
    qb = q.reshape(n, nb, BLOCK, hkv, g, dh)

    def with_prev(t):
        t = t.reshape(n, nb, BLOCK, hkv, dh)
        prev = jnp.concatenate([jnp.zeros_like(t[:, :1]), t[:, :-1]], axis=1)
        return jnp.concatenate([prev, t], axis=2)

    k2, v2 = with_prev(k), with_prev(v)
    qi = jnp.arange(BLOCK)[:, None]
    ki = jnp.arange(2 * BLOCK)[None, :]
    rel = qi + BLOCK - ki
    band = (rel >= 0) & (rel <= window_units)
    key_unit = jnp.arange(nb)[:, None] * BLOCK + jnp.arange(2 * BLOCK)[None, :] - BLOCK
    valid = band[None] & (key_unit >= n_invalid)[:, None, :]
    bias = rel_bias(table, jnp.maximum(rel, 0) * dilation).reshape(hkv, g, BLOCK, 2 * BLOCK)
    s = jnp.einsum('nbqhgd,nbkhd->nbhgqk', qb, k2, preferred_element_type=jnp.float32) * SCALE + bias
    p, lse = masked_softmax(s, valid[None, :, None, None])
    o = jnp.einsum('nbhgqk,nbkhd->nbqhgd', p.astype(v.dtype), v2).reshape(n, L, hq, dh)
    lse = lse.transpose(0, 1, 4, 2, 3).reshape(n, L, hq)
    return o, lse


def gathered_attention(q, kv_cat, window_units, dilation, table):
    n, t, hq, dh = q.shape
    hkv = kv_cat.shape[3]
    g = hq // hkv
    lb = kv_cat.shape[1] - t
    j = jnp.arange(window_units + 1)
    idx = lb + jnp.arange(t)[:, None] - j[None, :] * dilation
    valid = idx >= 0
    kvg = kv_cat[:, jnp.maximum(idx, 0)]
    bias = rel_bias(table, j * dilation).reshape(hkv, g, -1)
    qg = q.reshape(n, t, hkv, g, dh)
    s = jnp.einsum('nthgd,ntjhd->nthgj', qg, kvg[:, :, :, 0], preferred_element_type=jnp.float32) * SCALE
    s = s + bias[None, None]
    p, lse = masked_softmax(s, valid[None, :, None, None, :])
    o = jnp.einsum('nthgj,ntjhd->nthgd', p.astype(kv_cat.dtype), kvg[:, :, :, 1]).reshape(n, t, hq, dh)
    return o, lse.reshape(n, t, hq)


def sink_gate(o, lse, sinks):
    return o * jax.nn.sigmoid(lse - sinks.astype(jnp.float32)).astype(o.dtype)[..., None]


def combine_groups(outs, lses):
    w = jax.nn.softmax(jnp.stack(lses, 0), axis=0)
    return jnp.einsum('gnth,gnthd->nthd', w.astype(outs[0].dtype), jnp.stack(outs, 0))


def mixer_a_prompt(x, w_qkv, w_o, sinks, table):
    b, s, _ = x.shape
    qkv = x @ w_qkv
    q = qkv[..., :N_HEADS * HEAD_DIM].reshape(b, s, N_HEADS, HEAD_DIM)
    kv = qkv[..., N_HEADS * HEAD_DIM:].reshape(b, s, 2, N_KV_HEADS_A, HEAD_DIM)
    o, lse = banded_attention(q, kv[:, :, 0], kv[:, :, 1], 0, WINDOW_A, 1, table)
    y = sink_gate(o, lse, sinks).reshape(b, s, -1) @ w_o
    keep = min(WINDOW_A, s)
    return y, kv[:, s - keep:]


def mixer_a_sample(x, cache, w_qkv, w_o, sinks, table):
    n, t, _ = x.shape
    qkv = x @ w_qkv
    q = qkv[..., :N_HEADS * HEAD_DIM].reshape(n, t, N_HEADS, HEAD_DIM)
    kv = qkv[..., N_HEADS * HEAD_DIM:].reshape(n, t, 2, N_KV_HEADS_A, HEAD_DIM)
    kv_cat = jnp.concatenate([cache.astype(kv.dtype), kv], axis=1)
    o, lse = gathered_attention(q, kv_cat, WINDOW_A, 1, table)
    y = sink_gate(o, lse, sinks).reshape(n, t, -1) @ w_o
    return y, kv_cat[:, t:]


def mixer_b_prompt(x, w_qkv, w_o, table):
    b, s, _ = x.shape
    qkv = (x @ w_qkv).reshape(b, s, N_GROUPS_B, 3, N_HEADS, HEAD_DIM)
    outs, lses, states = [], [], []
    for gi, (w, d) in enumerate(DILATED_GROUPS):
        span = d * BLOCK
        s_pad = -(-s // span) * span
        pad = s_pad - s
        lsub = s_pad // d

        def to_strided(t):
            t = jnp.pad(t, ((0, 0), (pad, 0), (0, 0), (0, 0)))
            t = t.reshape(b, lsub, d, N_HEADS, HEAD_DIM).transpose(0, 2, 1, 3, 4)
            return t.reshape(b * d, lsub, N_HEADS, HEAD_DIM)

        o, lse = banded_attention(to_strided(qkv[:, :, gi, 0]), to_strided(qkv[:, :, gi, 1]),
                                  to_strided(qkv[:, :, gi, 2]), pad // d, w // d, d, table)
        o = o.reshape(b, d, lsub, N_HEADS, HEAD_DIM).transpose(0, 2, 1, 3, 4).reshape(b, s_pad, N_HEADS, HEAD_DIM)
        lse = lse.reshape(b, d, lsub, N_HEADS).transpose(0, 2, 1, 3).reshape(b, s_pad, N_HEADS)
        outs.append(o[:, pad:])
        lses.append(lse[:, pad:])
        keep = min(w, s)
        states.append(qkv[:, s - keep:, gi, 1:3])
    y = combine_groups(outs, lses).reshape(b, s, -1) @ w_o
    return y, states


def mixer_b_sample(x, caches, w_qkv, w_o, table):
    n, t, _ = x.shape
    qkv = (x @ w_qkv).reshape(n, t, N_GROUPS_B, 3, N_HEADS, HEAD_DIM)
    outs, lses, states = [], [], []
    for gi, (w, d) in enumerate(DILATED_GROUPS):
        kv_cat = jnp.concatenate([caches[gi].astype(qkv.dtype), qkv[:, :, gi, 1:3]], axis=1)
        o, lse = gathered_attention(qkv[:, :, gi, 0], kv_cat, w // d, d, table)
        outs.append(o)
        lses.append(lse)
        states.append(kv_cat[:, t:])
    y = combine_groups(outs, lses).reshape(n, t, -1) @ w_o
    return y, states


def setup_inputs(seed: int = 0) -> dict:
    key = jax.random.key(seed)
    ks = jax.random.split(key, 20)
    f32 = jnp.float32
    nrm = lambda k, shape, scale: jax.random.normal(k, shape, f32) * scale
    lb_a = min(WINDOW_A, PAST_LEN)
    lb_b = [min(w, PAST_LEN) for (w, _) in DILATED_GROUPS]
    return {
        "x_prompt": nrm(ks[0], (BATCH, SEQ, D_MODEL), 1.0),
        "x_sample": nrm(ks[1], (DEC_BATCH, DEC_SEQ, D_MODEL), 1.0),
        "cache_a_kv": nrm(ks[2], (N_LAYERS_A, DEC_BATCH, lb_a, 2, N_KV_HEADS_A, HEAD_DIM), 1.0),
        "cache_b1_kv": nrm(ks[3], (N_LAYERS_B, DEC_BATCH, lb_b[0], 2, N_HEADS, HEAD_DIM), 1.0),
        "cache_b2_kv": nrm(ks[4], (N_LAYERS_B, DEC_BATCH, lb_b[1], 2, N_HEADS, HEAD_DIM), 1.0),
        "cache_b3_kv": nrm(ks[5], (N_LAYERS_B, DEC_BATCH, lb_b[2], 2, N_HEADS, HEAD_DIM), 1.0),
        "rel_bias_table": nrm(ks[6], (NUM_BUCKETS, N_HEADS), 0.3),
        "ln_gain": 1.0 + nrm(ks[7], (DEPTH, 3, D_MODEL), 0.02),
        "ln_bias": nrm(ks[8], (DEPTH, 3, D_MODEL), 0.02),
        "ffn_w_gate": nrm(ks[9], (DEPTH, 2, D_MODEL, D_FF), D_MODEL ** -0.5),
        "ffn_w_up": nrm(ks[10], (DEPTH, 2, D_MODEL, D_FF), D_MODEL ** -0.5),
        "ffn_w_down": nrm(ks[11], (DEPTH, 2, D_FF, D_MODEL), BETA * D_FF ** -0.5),
        "attn_a_w_qkv": nrm(ks[12], (N_LAYERS_A, D_MODEL, QKV_A), D_MODEL ** -0.5),
        "attn_a_w_o": nrm(ks[13], (N_LAYERS_A, N_HEADS * HEAD_DIM, D_MODEL), BETA * (N_HEADS * HEAD_DIM) ** -0.5),
        "attn_a_sinks": nrm(ks[14], (N_LAYERS_A, N_HEADS), 0.5),
        "attn_b_w_qkv": nrm(ks[15], (N_LAYERS_B, D_MODEL, QKV_B), D_MODEL ** -0.5),
        "attn_b_w_o": nrm(ks[16], (N_LAYERS_B, N_HEADS * HEAD_DIM, D_MODEL), BETA * (N_HEADS * HEAD_DIM) ** -0.5),
    }


def reference(x_prompt, x_sample, cache_a_kv, cache_b1_kv, cache_b2_kv, cache_b3_kv, rel_bias_table,
              ln_gain, ln_bias, ffn_w_gate, ffn_w_up, ffn_w_down, attn_a_w_qkv, attn_a_w_o,
              attn_a_sinks, attn_b_w_qkv, attn_b_w_o):
    def post_norm(x, delta, i, j):
        return layer_norm(ALPHA * x + delta, ln_gain[i, j], ln_bias[i, j])

    def half_ffn(x, i, f, j):
        return post_norm(x, 0.5 * swiglu(x, ffn_w_gate[i, f], ffn_w_up[i, f], ffn_w_down[i, f]), i, j)

    xp, xs = x_prompt, x_sample
    a_p, a_s = [], []
    b1_p, b2_p, b3_p, b1_s, b2_s, b3_s = [], [], [], [], [], []
    for i in range(DEPTH):
        li = i // 2
        xp, xs = half_ffn(xp, i, 0, 0), half_ffn(xs, i, 0, 0)
        if i % 2 == 0:
            yp, sp = mixer_a_prompt(xp, attn_a_w_qkv[li], attn_a_w_o[li], attn_a_sinks[li], rel_bias_table)
            ys, ss = mixer_a_sample(xs, cache_a_kv[li], attn_a_w_qkv[li], attn_a_w_o[li], attn_a_sinks[li], rel_bias_table)
            a_p.append(sp)
            a_s.append(ss)
        else:
            yp, sps = mixer_b_prompt(xp, attn_b_w_qkv[li], attn_b_w_o[li], rel_bias_table)
            ys, sss = mixer_b_sample(xs, (cache_b1_kv[li], cache_b2_kv[li], cache_b3_kv[li]),
                                     attn_b_w_qkv[li], attn_b_w_o[li], rel_bias_table)
            b1_p.append(sps[0]); b2_p.append(sps[1]); b3_p.append(sps[2])
            b1_s.append(sss[0]); b2_s.append(sss[1]); b3_s.append(sss[2])
        xp, xs = post_norm(xp, yp, i, 1), post_norm(xs, ys, i, 1)
        xp, xs = half_ffn(xp, i, 1, 2), half_ffn(xs, i, 1, 2)
    return (xp, xs,
            jnp.stack(a_p, 0), jnp.stack(b1_p, 0), jnp.stack(b2_p, 0), jnp.stack(b3_p, 0),
            jnp.stack(a_s, 0), jnp.stack(b1_s, 0), jnp.stack(b2_s, 0), jnp.stack(b3_s, 0))
```

```python
import functools
import math

import jax
import jax.numpy as jnp
from jax import lax
from jax.experimental import pallas as pl
from jax.experimental.pallas import tpu as pltpu

D_MODEL = 1024
HEAD_DIM = 64
N_HEADS = 16
N_KV_HEADS_A = 4
WINDOW_A = 128
DILATED_GROUPS = ((128, 1), (512, 4), (2048, 16))
BLOCK = 128
NUM_BUCKETS = 32
MAX_DISTANCE = 2048
DEPTH = 2
ALPHA = (2 * DEPTH) ** 0.25
LN_EPS = 1e-5
NEG_INF = -1e30
SCALE = HEAD_DIM ** -0.5

LANES = 128
V7X_VMEM_BYTES = 64 * 1024 * 1024
VMEM_LIMIT = 56 * 1024 * 1024

F32 = jnp.float32
BF16 = jnp.bfloat16


def _cparams(n_axes, vmem=VMEM_LIMIT):
    return pltpu.CompilerParams(dimension_semantics=("arbitrary",) * n_axes, vmem_limit_bytes=vmem)


def _resident(shape):
    nd = len(shape)
    return pl.BlockSpec(shape, lambda *_: (0,) * nd, pipeline_mode=pl.Buffered(1))


def _dot(a, b):
    return jnp.dot(a, b, preferred_element_type=F32)


def _dot_nt(a, b):
    return lax.dot_general(a, b, (((1,), (1,)), ((), ())), preferred_element_type=F32)


def _layer_norm(y, g, b):
    mu = jnp.mean(y, axis=-1, keepdims=True)
    yc = y - mu
    var = jnp.mean(yc * yc, axis=-1, keepdims=True)
    return yc * lax.rsqrt(var + LN_EPS) * g + b


FFN_CHUNK = 256


def _ffn_ln_kernel(x_ref, wg_ref, wu_ref, wd_ref, g_ref, b_ref, o_ref):
    x = x_ref[...]
    xb = x.astype(BF16)
    d_ff = wg_ref.shape[1]
    acc = jnp.zeros(x.shape, F32)
    for c in range(d_ff // FFN_CHUNK):
        sl = slice(c * FFN_CHUNK, (c + 1) * FFN_CHUNK)
        gate = _dot(xb, wg_ref[:, sl])
        up = _dot(xb, wu_ref[:, sl])
        hid = gate * jax.nn.sigmoid(gate) * up
        acc = acc + _dot(hid.astype(BF16), wd_ref[sl, :])
    o_ref[...] = _layer_norm(ALPHA * x + 0.5 * acc, g_ref[...], b_ref[...])


def ffn_ln(x, wg, wu, wd, gain, bias, tm=512):
    rows, d = x.shape
    d_ff = wg.shape[1]
    tm = min(tm, rows)
    return pl.pallas_call(
        _ffn_ln_kernel,
        grid=(rows // tm,),
        in_specs=[
            pl.BlockSpec((tm, d), lambda i: (i, 0)),
            _resident((d, d_ff)), _resident((d, d_ff)), _resident((d_ff, d)),
            _resident((1, d)), _resident((1, d)),
        ],
        out_specs=pl.BlockSpec((tm, d), lambda i: (i, 0)),
        out_shape=jax.ShapeDtypeStruct((rows, d), F32),
        compiler_params=_cparams(1),
        name="ffn_ln",
    )(x, wg, wu, wd, gain, bias)


def _proj_kernel(*refs, with_q):
    if with_q:
        x_ref, wq_ref, wkvt_ref, q_ref, kvt_ref = refs
    else:
        x_ref, wkvt_ref, kvt_ref = refs
    xb = x_ref[0].astype(BF16)
    if with_q:
        q_ref[0, 0] = _dot(xb, wq_ref[...]).astype(q_ref.dtype)
    kvt_ref[0, 0] = _dot_nt(wkvt_ref[...], xb).astype(kvt_ref.dtype)


def project(x3, wq, wkvt, *, d, tl, l_off, l_len, kv_dtype, name, q_dtype=BF16):
    bsz, s, dm = x3.shape
    lsub = s // d
    xv = x3.reshape(bsz, lsub, d * dm)
    c2 = wkvt.shape[0]
    nl = l_len // tl
    off = l_off // tl
    with_q = wq is not None
    in_specs = [pl.BlockSpec((1, tl, dm), lambda b, r, j: (b, j + off, r))]
    args = [xv]
    out_specs, out_shape = [], []
    if with_q:
        cq = wq.shape[1]
        in_specs.append(_resident(wq.shape))
        args.append(wq)
        out_specs.append(pl.BlockSpec((1, 1, tl, cq), lambda b, r, j: (b, r, j, 0)))
        out_shape.append(jax.ShapeDtypeStruct((bsz, d, l_len, cq), q_dtype))
    in_specs.append(_resident(wkvt.shape))
    args.append(wkvt)
    out_specs.append(pl.BlockSpec((1, 1, c2, tl), lambda b, r, j: (b, r, 0, j)))
    out_shape.append(jax.ShapeDtypeStruct((bsz, d, c2, l_len), kv_dtype))
    return pl.pallas_call(
        functools.partial(_proj_kernel, with_q=with_q),
        grid=(bsz, d, nl),
        in_specs=in_specs, out_specs=out_specs, out_shape=out_shape,
        compiler_params=_cparams(3),
        name=name,
    )(*args)


def _bias_kernel(table_ref, idx_ref, o_ref):
    h = pl.program_id(0)
    idx = idx_ref[...]
    acc = jnp.full(idx.shape, NEG_INF, F32)
    for b in range(NUM_BUCKETS):
        acc = jnp.where(idx == b, table_ref[b * N_HEADS + h], acc)
    o_ref[0] = acc


def bias_lookup(table, idx):
    r, c = idx.shape
    return pl.pallas_call(
        _bias_kernel,
        grid_spec=pltpu.PrefetchScalarGridSpec(
            num_scalar_prefetch=1,
            grid=(N_HEADS,),
            in_specs=[pl.BlockSpec((r, c), lambda h, t: (0, 0))],
            out_specs=pl.BlockSpec((1, r, c), lambda h, t: (h, 0, 0)),
        ),
        out_shape=jax.ShapeDtypeStruct((N_HEADS, r, c), F32),
        name="bias_lookup",
    )(table.reshape(-1), idx)


def _t5_bucket(dist):
    max_exact = NUM_BUCKETS // 2
    d = jnp.maximum(dist.astype(F32), 1.0)
    large = max_exact + (jnp.log(d / max_exact) / math.log(MAX_DISTANCE / max_exact)
                         * (NUM_BUCKETS - max_exact)).astype(jnp.int32)
    large = jnp.minimum(large, NUM_BUCKETS - 1)
    return jnp.where(dist < max_exact, dist, large)


def _prompt_bias_idx(dilation, window_units):
    qi = jnp.arange(BLOCK)[:, None]
    ki = jnp.arange(2 * BLOCK)[None, :]
    rel = qi + BLOCK - ki
    band = (rel >= 0) & (rel <= window_units)
    return jnp.where(band, _t5_bucket(jnp.maximum(rel, 0) * dilation), -1).astype(jnp.int32)


def _sample_bias_idx(dilation, window_units, lb, t):
    tau = jnp.arange(t)[:, None]
    pos = jnp.arange(lb)[None, :]
    dist = lb + tau - pos
    ok = (dist % dilation == 0) & (dist // dilation <= window_units)
    idx_c = jnp.where(ok, _t5_bucket(dist), -1).astype(jnp.int32)
    nu = jnp.arange(LANES)[None, :] - (LANES - t)
    dist_n = tau - nu
    ok_n = (nu >= 0) & (dist_n >= 0) & (dist_n % dilation == 0) & (dist_n // dilation <= window_units)
    idx_n = jnp.where(ok_n, _t5_bucket(jnp.maximum(dist_n, 0)), -1).astype(jnp.int32)
    return idx_c, idx_n


def _band_attn_kernel(*refs, n_kv_heads, n_blocks, with_sinks):
    if with_sinks:
        sink_ref, q_ref, kvt_ref, bias_ref, o_ref = refs
    else:
        q_ref, kvt_ref, bias_ref, o_ref, lse_ref = refs
    i = pl.program_id(1)
    ck = n_kv_heads * HEAD_DIM
    group = N_HEADS // n_kv_heads
    cur = pl.multiple_of(i * BLOCK, BLOCK)
    prev = pl.multiple_of(jnp.maximum(i - 1, 0) * BLOCK, BLOCK)
    no_prev = jnp.where(i == 0, NEG_INF, 0.0).astype(F32)
    for h in range(N_HEADS):
        kh = h // group
        qh = q_ref[0, :, h * HEAD_DIM:(h + 1) * HEAD_DIM]
        k_rows = slice(kh * HEAD_DIM, (kh + 1) * HEAD_DIM)
        v_rows = slice(ck + kh * HEAD_DIM, ck + (kh + 1) * HEAD_DIM)
        s_c = _dot(qh, kvt_ref[0, k_rows, pl.ds(cur, BLOCK)]) + bias_ref[h, :, BLOCK:]
        m = jnp.max(s_c, axis=-1, keepdims=True)
        if n_blocks > 1:
            s_p = _dot(qh, kvt_ref[0, k_rows, pl.ds(prev, BLOCK)]) + (bias_ref[h, :, :BLOCK] + no_prev)
            m = jnp.maximum(m, jnp.max(s_p, axis=-1, keepdims=True))
        e_c = jnp.exp(s_c - m)
        den = jnp.sum(e_c, axis=-1, keepdims=True)
        acc = _dot_nt(e_c.astype(BF16), kvt_ref[0, v_rows, pl.ds(cur, BLOCK)])
        if n_blocks > 1:
            e_p = jnp.exp(s_p - m)
            den = den + jnp.sum(e_p, axis=-1, keepdims=True)
            acc = acc + _dot_nt(e_p.astype(BF16), kvt_ref[0, v_rows, pl.ds(prev, BLOCK)])
        out = acc / den
        lse = m + jnp.log(den)
        if with_sinks:
            out = out * jax.nn.sigmoid(lse - sink_ref[h])
        else:
            lse_ref[0, :, h * HEAD_DIM:(h + 1) * HEAD_DIM] = jnp.broadcast_to(lse, (BLOCK, HEAD_DIM))
        o_ref[0, :, h * HEAD_DIM:(h + 1) * HEAD_DIM] = out


def band_attention(q, kvt, bias, sinks, *, bsz, d, name):
    _, _, lsub, c = q.shape
    c2 = kvt.shape[2]
    nb = lsub // BLOCK
    q = q.reshape(bsz * d, lsub, c)
    kvt = kvt.reshape(bsz * d, c2, lsub)
    with_sinks = sinks is not None
    kern = functools.partial(_band_attn_kernel, n_kv_heads=c2 // (2 * HEAD_DIM), n_blocks=nb,
                             with_sinks=with_sinks)
    o_spec = pl.BlockSpec((1, BLOCK, c), lambda s, i, *_: (s // d, i, s % d))
    o_shape = jax.ShapeDtypeStruct((bsz, lsub, d * c), F32)
    in_specs = [
        pl.BlockSpec((1, BLOCK, c), lambda s, i, *_: (s, i, 0)),
        pl.BlockSpec((1, c2, lsub), lambda s, i, *_: (s, 0, 0)),
        pl.BlockSpec(bias.shape, lambda s, i, *_: (0, 0, 0), pipeline_mode=pl.Buffered(1)),
    ]
    if with_sinks:
        return pl.pallas_call(
            kern,
            grid_spec=pltpu.PrefetchScalarGridSpec(
                num_scalar_prefetch=1, grid=(bsz * d, nb), in_specs=in_specs, out_specs=o_spec),
            out_shape=o_shape, compiler_params=_cparams(2), name=name,
        )(sinks, q, kvt, bias)
    return pl.pallas_call(
        kern, grid=(bsz * d, nb), in_specs=in_specs, out_specs=[o_spec, o_spec],
        out_shape=[o_shape, o_shape], compiler_params=_cparams(2), name=name,
    )(q, kvt, bias)


def _window_attn_kernel(*refs, heads_per_step, group, t, lb, with_sinks):
    if with_sinks:
        sink_ref, q_ref, cache_ref, new_ref, bias_c_ref, bias_n_ref, o_ref, state_ref = refs
    else:
        q_ref, cache_ref, new_ref, bias_c_ref, bias_n_ref, o_ref, lse_ref, state_ref = refs
    n = pl.program_id(0)
    hc = pl.program_id(1)
    rows = heads_per_step * HEAD_DIM
    shift = (LANES - t) - (n % (LANES // t)) * t
    new_k = pltpu.roll(new_ref[0], shift, 1)
    new_v = pltpu.roll(new_ref[1], shift, 1)
    new_kb, new_vb = new_k.astype(BF16), new_v.astype(BF16)
    for kh in range(heads_per_step):
        r = slice(kh * HEAD_DIM, (kh + 1) * HEAD_DIM)
        kt = cache_ref[0, 0, r, :].astype(BF16)
        vt = cache_ref[0, 1, r, :].astype(BF16)
        for g in range(group):
            hl = kh * group + g
            qh = q_ref[0, :, hl * HEAD_DIM:(hl + 1) * HEAD_DIM].astype(BF16)
            s_c = _dot(qh, kt) + bias_c_ref[hl]
            s_n = _dot(qh, new_kb[r]) + bias_n_ref[hl]
            m = jnp.maximum(jnp.max(s_c, axis=-1, keepdims=True), jnp.max(s_n, axis=-1, keepdims=True))
            e_c = jnp.exp(s_c - m)
            e_n = jnp.exp(s_n - m)
            den = jnp.sum(e_c, axis=-1, keepdims=True) + jnp.sum(e_n, axis=-1, keepdims=True)
            out = _dot_nt((e_c / den).astype(BF16), vt) + _dot_nt((e_n / den).astype(BF16), new_vb[r])
            lse = m + jnp.log(den)
            if with_sinks:
                h_abs = hc * (heads_per_step * group) + hl
                out = out * jax.nn.sigmoid(lse - sink_ref[h_abs])
            else:
                lse_ref[0, :, hl * HEAD_DIM:(hl + 1) * HEAD_DIM] = jnp.broadcast_to(lse, (t, HEAD_DIM))
            o_ref[0, :, hl * HEAD_DIM:(hl + 1) * HEAD_DIM] = out
    lane = lax.broadcasted_iota(jnp.int32, (rows, LANES), 1)
    keep = lane < LANES - t
    n_tiles = lb // LANES
    for kv, new in ((0, new_k), (1, new_v)):
        rolled = pltpu.roll(cache_ref[0, kv, :, 0:LANES], LANES - t, 1)
        for j in range(n_tiles):
            if j + 1 < n_tiles:
                nxt = pltpu.roll(cache_ref[0, kv, :, (j + 1) * LANES:(j + 2) * LANES], LANES - t, 1)
            else:
                nxt = new
            state_ref[0, kv, :, j * LANES:(j + 1) * LANES] = jnp.where(keep, rolled, nxt)
            rolled = nxt


def window_attention(q, cache_t, new_t, bias_c, bias_n, sinks, *, heads_per_step, name):
    n_s, t, cq = q.shape
    _, _, ck, lb = cache_t.shape
    n_kv = ck // HEAD_DIM
    group = (cq // HEAD_DIM) // n_kv
    n_hc = n_kv // heads_per_step
    rows = heads_per_step * HEAD_DIM
    cq_step = rows * group
    per_tile = LANES // t
    with_sinks = sinks is not None
    kern = functools.partial(_window_attn_kernel, heads_per_step=heads_per_step, group=group, t=t, lb=lb,
                             with_sinks=with_sinks)
    o_spec = pl.BlockSpec((1, t, cq_step), lambda n, hc, *_: (n, 0, hc))
    o_shape = jax.ShapeDtypeStruct((n_s, t, cq), F32)
    st_spec = pl.BlockSpec((1, 2, rows, lb), lambda n, hc, *_: (n, 0, hc, 0))
    st_shape = jax.ShapeDtypeStruct(cache_t.shape, F32)
    in_specs = [
        pl.BlockSpec((1, t, cq_step), lambda n, hc, *_: (n, 0, hc)),
        pl.BlockSpec((1, 2, rows, lb), lambda n, hc, *_: (n, 0, hc, 0)),
        pl.BlockSpec((2, rows, LANES), lambda n, hc, *_: (0, hc, n // per_tile)),
        pl.BlockSpec((heads_per_step * group, t, lb), lambda n, hc, *_: (hc, 0, 0)),
        pl.BlockSpec((heads_per_step * group, t, LANES), lambda n, hc, *_: (hc, 0, 0)),
    ]
    if with_sinks:
        return pl.pallas_call(
            kern,
            grid_spec=pltpu.PrefetchScalarGridSpec(
                num_scalar_prefetch=1, grid=(n_s, n_hc), in_specs=in_specs, out_specs=[o_spec, st_spec]),
            out_shape=[o_shape, st_shape], compiler_params=_cparams(2), name=name,
        )(sinks, q, cache_t, new_t, bias_c, bias_n)
    return pl.pallas_call(
        kern, grid=(n_s, n_hc), in_specs=in_specs, out_specs=[o_spec, o_spec, st_spec],
        out_shape=[o_shape, o_shape, st_shape], compiler_params=_cparams(2), name=name,
    )(q, cache_t, new_t, bias_c, bias_n)


def _mix_out_kernel(*refs, n_groups):
    o_refs = refs[:n_groups]
    lse_refs = refs[n_groups:2 * n_groups] if n_groups > 1 else ()
    x_ref, wo_ref, g_ref, b_ref, out_ref = refs[len(o_refs) + len(lse_refs):]
    if n_groups == 1:
        mixed = o_refs[0][...]
    else:
        lses = [r[...] for r in lse_refs]
        m = functools.reduce(jnp.maximum, lses)
        es = [jnp.exp(l - m) for l in lses]
        tot = functools.reduce(lambda a, b: a + b, es)
        mixed = functools.reduce(lambda a, b: a + b, [(e / tot) * r[...] for e, r in zip(es, o_refs)])
    y = _dot(mixed.astype(BF16), wo_ref[...])
    out_ref[...] = _layer_norm(ALPHA * x_ref[...] + y, g_ref[...], b_ref[...])


def mix_out(outs, lses, x, wo, gain, bias, tm=512):
    rows, d = x.shape
    tm = min(tm, rows)
    ng = len(outs)
    row_spec = pl.BlockSpec((tm, d), lambda i: (i, 0))
    n_row_in = ng + len(lses) + 1
    return pl.pallas_call(
        functools.partial(_mix_out_kernel, n_groups=ng),
        grid=(rows // tm,),
        in_specs=[row_spec] * n_row_in + [_resident(wo.shape), _resident((1, d)), _resident((1, d))],
        out_specs=row_spec,
        out_shape=jax.ShapeDtypeStruct((rows, d), F32),
        compiler_params=_cparams(1),
        name="mix_out",
    )(*outs, *lses, x, wo, gain, bias)


def _to_pos_minor(cache):
    n, lb, _, h, dh = cache.shape
    return jnp.transpose(cache, (0, 2, 3, 4, 1)).reshape(n, 2, h * dh, lb)


def _from_pos_minor(state_t, n_heads):
    n, _, _, lb = state_t.shape
    return jnp.transpose(state_t.reshape(n, 2, n_heads, HEAD_DIM, lb), (0, 4, 1, 2, 3))[None]


def kernel(x_prompt, x_sample, cache_a_kv, cache_b1_kv, cache_b2_kv, cache_b3_kv, rel_bias_table, ln_gain, ln_bias, ffn_w_gate, ffn_w_up, ffn_w_down, attn_a_w_qkv, attn_a_w_o, attn_a_sinks, attn_b_w_qkv, attn_b_w_o):
    bsz, seq, dm = x_prompt.shape
    n_s, t, _ = x_sample.shape
    hd = N_HEADS * HEAD_DIM
    ck_a = N_KV_HEADS_A * HEAD_DIM
    xp = x_prompt.reshape(bsz * seq, dm)
    xs = x_sample.reshape(n_s * t, dm)
    table = rel_bias_table.astype(F32)

    def ln_params(i, j):
        return ln_gain[i, j].reshape(1, dm).astype(F32), ln_bias[i, j].reshape(1, dm).astype(F32)

    def half_ffn(x, i, f, j):
        g, b = ln_params(i, j)
        return ffn_ln(x, ffn_w_gate[i, f].astype(BF16), ffn_w_up[i, f].astype(BF16),
                      ffn_w_down[i, f].astype(BF16), g, b)

    def split_qkv(w, q_cols, kv_cols):
        wq = (w[:, q_cols[0]:q_cols[1]] * SCALE).astype(BF16)
        wkvt = w[:, kv_cols[0]:kv_cols[1]].T.astype(BF16)
        return wq, wkvt

    li = 0
    xp = half_ffn(xp, 0, 0, 0)
    xs = half_ffn(xs, 0, 0, 0)
    wq, wkvt = split_qkv(attn_a_w_qkv[li], (0, hd), (hd, hd + 2 * ck_a))
    wo = attn_a_w_o[li].astype(BF16)
    sinks = attn_a_sinks[li].astype(F32)
    g1, b1 = ln_params(0, 1)
    keep = min(WINDOW_A, seq)

    xp3 = xp.reshape(bsz, seq, dm)
    q, kvt = project(xp3, wq, wkvt, d=1, tl=512, l_off=0, l_len=seq, kv_dtype=BF16, name="proj_a")
    (st,) = project(xp3, None, wkvt, d=1, tl=keep, l_off=seq - keep, l_len=keep, kv_dtype=F32, name="state_a")
    state_a_p = _from_pos_minor(st.reshape(bsz, 2, ck_a, keep), N_KV_HEADS_A)
    bias_p = bias_lookup(table, _prompt_bias_idx(1, WINDOW_A))
    o = band_attention(q, kvt, bias_p, sinks, bsz=bsz, d=1, name="attn_a")
    xp = mix_out([o.reshape(bsz * seq, hd)], [], xp, wo, g1, b1)

    xs3 = xs.reshape(1, n_s * t, dm)
    q, new_t = project(xs3, wq, wkvt, d=1, tl=n_s * t, l_off=0, l_len=n_s * t, kv_dtype=F32, q_dtype=F32,
                       name="proj_a_s")
    idx_c, idx_n = _sample_bias_idx(1, WINDOW_A, cache_a_kv.shape[2], t)
    o, state_a_s = window_attention(
        q.reshape(n_s, t, hd), _to_pos_minor(cache_a_kv[li]), new_t.reshape(2, ck_a, n_s * t),
        bias_lookup(table, idx_c), bias_lookup(table, idx_n), sinks,
        heads_per_step=N_KV_HEADS_A, name="win_a")
    state_a_s = _from_pos_minor(state_a_s, N_KV_HEADS_A)
    xs = mix_out([o.reshape(n_s * t, hd)], [], xs, wo, g1, b1)

    xp = half_ffn(xp, 0, 1, 2)
    xs = half_ffn(xs, 0, 1, 2)

    xp = half_ffn(xp, 1, 0, 0)
    xs = half_ffn(xs, 1, 0, 0)
    w_qkv = attn_b_w_qkv[li]
    wo = attn_b_w_o[li].astype(BF16)
    g1, b1 = ln_params(1, 1)
    caches = (cache_b1_kv[li], cache_b2_kv[li], cache_b3_kv[li])
    xp3 = xp.reshape(bsz, seq, dm)
    xs3 = xs.reshape(1, n_s * t, dm)
    outs_p, lses_p, outs_s, lses_s, states_p, states_s = [], [], [], [], [], []
    for gi, (w, d) in enumerate(DILATED_GROUPS):
        base = gi * 3 * hd
        wq, wkvt = split_qkv(w_qkv, (base, base + hd), (base + hd, base + 3 * hd))
        lsub = seq // d
        units = w // d
        q, kvt = project(xp3, wq, wkvt, d=d, tl=min(lsub, 512), l_off=0, l_len=lsub, kv_dtype=BF16,
                         name=f"proj_b{gi}")
        keep = min(w, seq)
        (st,) = project(xp3, None, wkvt, d=1, tl=min(keep, 512), l_off=seq - keep, l_len=keep, kv_dtype=F32,
                        name=f"state_b{gi}")
        states_p.append(_from_pos_minor(st.reshape(bsz, 2, hd, keep), N_HEADS))
        bias_p = bias_lookup(table, _prompt_bias_idx(d, units))
        o, lse = band_attention(q, kvt, bias_p, None, bsz=bsz, d=d, name=f"attn_b{gi}")
        outs_p.append(o.reshape(bsz * seq, hd))
        lses_p.append(lse.reshape(bsz * seq, hd))
        q, new_t = project(xs3, wq, wkvt, d=1, tl=n_s * t, l_off=0, l_len=n_s * t, kv_dtype=F32,
                           q_dtype=F32, name=f"proj_b{gi}_s")
        lb = caches[gi].shape[1]
        idx_c, idx_n = _sample_bias_idx(d, units, lb, t)
        o, lse, st = window_attention(
            q.reshape(n_s, t, hd), _to_pos_minor(caches[gi]), new_t.reshape(2, hd, n_s * t),
            bias_lookup(table, idx_c), bias_lookup(table, idx_n), None,
            heads_per_step=max(1, min(N_HEADS, (4 * 1024 * 1024) // (2 * HEAD_DIM * lb * 4))),
            name=f"win_b{gi}")
        outs_s.append(o.reshape(n_s * t, hd))
        lses_s.append(lse.reshape(n_s * t, hd))
        states_s.append(_from_pos_minor(st, N_HEADS))
    xp = mix_out(outs_p, lses_p, xp, wo, g1, b1)
    xs = mix_out(outs_s, lses_s, xs, wo, g1, b1)
    xp = half_ffn(xp, 1, 1, 2)
    xs = half_ffn(xs, 1, 1, 2)

    return (xp.reshape(bsz, seq, dm), xs.reshape(n_s, t, dm),
            state_a_p, states_p[0], states_p[1], states_p[2],
            state_a_s, states_s[0], states_s[1], states_s[2])
```

```python
import functools
import math

import jax
import jax.numpy as jnp
from jax import lax
from jax.experimental import pallas as pl
from jax.experimental.pallas import tpu as pltpu

D_MODEL = 1024
HEAD_DIM = 64
N_HEADS = 16
N_KV_HEADS_A = 4
WINDOW_A = 128
DILATED_GROUPS = ((128, 1), (512, 4), (2048, 16))
BLOCK = 128
NUM_BUCKETS = 32
MAX_DISTANCE = 2048
DEPTH = 2
ALPHA = (2 * DEPTH) ** 0.25
LN_EPS = 1e-5
NEG_INF = -1e30
SCALE = HEAD_DIM ** -0.5

LANES = 128
N_CHUNKS = D_MODEL // LANES
V7X_VMEM_BYTES = 64 * 1024 * 1024
VMEM_LIMIT = 56 * 1024 * 1024

F32 = jnp.float32
BF16 = jnp.bfloat16


def _cparams(n_axes, vmem=VMEM_LIMIT):
    return pltpu.CompilerParams(dimension_semantics=("arbitrary",) * n_axes, vmem_limit_bytes=vmem)


def _resident(shape):
    nd = len(shape)
    return pl.BlockSpec(shape, lambda *_: (0,) * nd, pipeline_mode=pl.Buffered(1))


def _dot(a, b):
    return jnp.dot(a, b, preferred_element_type=F32)


def _dot_nt(a, b):
    return lax.dot_general(a, b, (((1,), (1,)), ((), ())), preferred_element_type=F32)


def _layer_norm(y, g, b):
    mu = jnp.mean(y, axis=-1, keepdims=True)
    yc = y - mu
    var = jnp.mean(yc * yc, axis=-1, keepdims=True)
    return yc * lax.rsqrt(var + LN_EPS) * g + b


FFN_CHUNK = 256


def _ffn_ln_kernel(x_ref, wg_ref, wu_ref, wd_ref, g_ref, b_ref, o_ref, *chunked_ref):
    x = x_ref[...]
    xb = x.astype(BF16)
    d_ff = wg_ref.shape[1]
    acc = jnp.zeros(x.shape, F32)
    for c in range(d_ff // FFN_CHUNK):
        sl = slice(c * FFN_CHUNK, (c + 1) * FFN_CHUNK)
        gate = _dot(xb, wg_ref[:, sl])
        up = _dot(xb, wu_ref[:, sl])
        hid = gate * jax.nn.sigmoid(gate) * up
        acc = acc + _dot(hid.astype(BF16), wd_ref[sl, :])
    y = _layer_norm(ALPHA * x + 0.5 * acc, g_ref[...], b_ref[...])
    o_ref[...] = y
    if chunked_ref:
        _store_chunked(chunked_ref[0], y)


def _store_chunked(ref, y, rows=None):
    for c in range(N_CHUNKS):
        if rows is None:
            ref[0, c] = y[:, c * LANES:(c + 1) * LANES]
        else:
            ref[0, c, rows, :] = y[:, c * LANES:(c + 1) * LANES]


def _load_chunked(ref, rows=None):
    if rows is None:
        return jnp.concatenate([ref[0, c] for c in range(N_CHUNKS)], axis=1)
    return jnp.concatenate([ref[0, c, rows, :] for c in range(N_CHUNKS)], axis=1)


def ffn_ln(x, wg, wu, wd, gain, bias, tm=512, chunked_seq=None):
    rows, d = x.shape
    d_ff = wg.shape[1]
    tm = min(tm, rows)
    out_specs = [pl.BlockSpec((tm, d), lambda i: (i, 0))]
    out_shape = [jax.ShapeDtypeStruct((rows, d), F32)]
    if chunked_seq is not None:
        per_seq = chunked_seq // tm
        out_specs.append(pl.BlockSpec((1, N_CHUNKS, tm, LANES), lambda i: (i // per_seq, 0, i % per_seq, 0)))
        out_shape.append(jax.ShapeDtypeStruct((rows // chunked_seq, N_CHUNKS, chunked_seq, LANES), F32))
    res = pl.pallas_call(
        _ffn_ln_kernel,
        grid=(rows // tm,),
        in_specs=[
            pl.BlockSpec((tm, d), lambda i: (i, 0)),
            _resident((d, d_ff)), _resident((d, d_ff)), _resident((d_ff, d)),
            _resident((1, d)), _resident((1, d)),
        ],
        out_specs=out_specs,
        out_shape=out_shape,
        compiler_params=_cparams(1),
        name="ffn_ln",
    )(x, wg, wu, wd, gain, bias)
    return res if chunked_seq is not None else res[0]


def _proj_kernel(*refs, with_q, d, lsub, res_per_step):
    if with_q:
        x_ref, wq_ref, wkvt_ref, q_ref, kvt_ref = refs
    else:
        x_ref, wkvt_ref, kvt_ref = refs
    if d == 1:
        xb = x_ref[0].astype(BF16)
    else:
        r0 = pl.program_id(1) * res_per_step
        xb = jnp.concatenate(
            [_load_chunked(x_ref, pl.ds(r0 + rr, lsub, stride=d)) for rr in range(res_per_step)],
            axis=0).astype(BF16)
    if with_q:
        q = _dot(xb, wq_ref[...]).astype(q_ref.dtype)
        for rr in range(res_per_step):
            q_ref[0, rr] = q[rr * lsub:(rr + 1) * lsub]
    kvt = _dot_nt(wkvt_ref[...], xb).astype(kvt_ref.dtype)
    for rr in range(res_per_step):
        kvt_ref[0, rr] = kvt[:, rr * lsub:(rr + 1) * lsub]


def project(x3, wq, wkvt, *, tl, l_off, l_len, kv_dtype, name, q_dtype=BF16):
    bsz, s, dm = x3.shape
    c2 = wkvt.shape[0]
    nl = l_len // tl
    off = l_off // tl
    with_q = wq is not None
    in_specs = [pl.BlockSpec((1, tl, dm), lambda b, j: (b, j + off, 0))]
    args = [x3]
    out_specs, out_shape = [], []
    if with_q:
        cq = wq.shape[1]
        in_specs.append(_resident(wq.shape))
        args.append(wq)
        out_specs.append(pl.BlockSpec((1, 1, tl, cq), lambda b, j: (b, 0, j, 0)))
        out_shape.append(jax.ShapeDtypeStruct((bsz, 1, l_len, cq), q_dtype))
    in_specs.append(_resident(wkvt.shape))
    args.append(wkvt)
    out_specs.append(pl.BlockSpec((1, 1, c2, tl), lambda b, j: (b, 0, 0, j)))
    out_shape.append(jax.ShapeDtypeStruct((bsz, 1, c2, l_len), kv_dtype))
    return pl.pallas_call(
        functools.partial(_proj_kernel, with_q=with_q, d=1, lsub=tl, res_per_step=1),
        grid=(bsz, nl),
        in_specs=in_specs, out_specs=out_specs, out_shape=out_shape,
        compiler_params=_cparams(2),
        name=name,
    )(*args)


def project_strided(xc, wq, wkvt, *, d, rows_per_step, name):
    bsz, _, s, _ = xc.shape
    lsub = s // d
    rps = max(1, rows_per_step // lsub)
    c2, cq = wkvt.shape[0], wq.shape[1]
    return pl.pallas_call(
        functools.partial(_proj_kernel, with_q=True, d=d, lsub=lsub, res_per_step=rps),
        grid=(bsz, d // rps),
        in_specs=[pl.BlockSpec((1, N_CHUNKS, s, LANES), lambda b, j: (b, 0, 0, 0)),
                  _resident(wq.shape), _resident(wkvt.shape)],
        out_specs=[pl.BlockSpec((1, rps, lsub, cq), lambda b, j: (b, j, 0, 0)),
                   pl.BlockSpec((1, rps, c2, lsub), lambda b, j: (b, j, 0, 0))],
        out_shape=[jax.ShapeDtypeStruct((bsz, d, lsub, cq), BF16),
                   jax.ShapeDtypeStruct((bsz, d, c2, lsub), BF16)],
        compiler_params=_cparams(2),
        name=name,
    )(xc, wq, wkvt)


def _bias_kernel(table_ref, idx_ref, o_ref):
    h = pl.program_id(0)
    idx = idx_ref[...]
    acc = jnp.full(idx.shape, NEG_INF, F32)
    for b in range(NUM_BUCKETS):
        acc = jnp.where(idx == b, table_ref[b * N_HEADS + h], acc)
    o_ref[0] = acc


def bias_lookup(table, idx):
    r, c = idx.shape
    return pl.pallas_call(
        _bias_kernel,
        grid_spec=pltpu.PrefetchScalarGridSpec(
            num_scalar_prefetch=1,
            grid=(N_HEADS,),
            in_specs=[pl.BlockSpec((r, c), lambda h, t: (0, 0))],
            out_specs=pl.BlockSpec((1, r, c), lambda h, t: (h, 0, 0)),
        ),
        out_shape=jax.ShapeDtypeStruct((N_HEADS, r, c), F32),
        name="bias_lookup",
    )(table.reshape(-1), idx)


def _t5_bucket(dist):
    max_exact = NUM_BUCKETS // 2
    d = jnp.maximum(dist.astype(F32), 1.0)
    large = max_exact + (jnp.log(d / max_exact) / math.log(MAX_DISTANCE / max_exact)
                         * (NUM_BUCKETS - max_exact)).astype(jnp.int32)
    large = jnp.minimum(large, NUM_BUCKETS - 1)
    return jnp.where(dist < max_exact, dist, large)


def _prompt_bias_idx(dilation, window_units):
    qi = jnp.arange(BLOCK)[:, None]
    ki = jnp.arange(2 * BLOCK)[None, :]
    rel = qi + BLOCK - ki
    band = (rel >= 0) & (rel <= window_units)
    return jnp.where(band, _t5_bucket(jnp.maximum(rel, 0) * dilation), -1).astype(jnp.int32)


def _sample_bias_idx(dilation, window_units, lb, t):
    tau = jnp.arange(t)[:, None]
    pos = jnp.arange(lb)[None, :]
    dist = lb + tau - pos
    ok = (dist % dilation == 0) & (dist // dilation <= window_units)
    idx_c = jnp.where(ok, _t5_bucket(dist), -1).astype(jnp.int32)
    nu = jnp.arange(LANES)[None, :] - (LANES - t)
    dist_n = tau - nu
    ok_n = (nu >= 0) & (dist_n >= 0) & (dist_n % dilation == 0) & (dist_n // dilation <= window_units)
    idx_n = jnp.where(ok_n, _t5_bucket(jnp.maximum(dist_n, 0)), -1).astype(jnp.int32)
    return jnp.concatenate([idx_c, idx_n], axis=1)


def _band_attn_kernel(*refs, n_kv_heads, n_blocks, d, with_sinks):
    if with_sinks:
        sink_ref, q_ref, kvt_ref, bias_ref, o_ref, s_scr, e_scr, f_scr, fac_scr = refs
    else:
        q_ref, kvt_ref, bias_ref, o_ref, lse_ref, s_scr, e_scr, f_scr, fac_scr, l_scr = refs
    r = pl.program_id(1)
    i = pl.program_id(2)
    ck = n_kv_heads * HEAD_DIM
    group = N_HEADS // n_kv_heads

    lane_lo = lax.broadcasted_iota(jnp.int32, (BLOCK, 2 * HEAD_DIM), 1) < HEAD_DIM

    def pair_tile(a, b):
        return jnp.where(lane_lo, a, b)

    def run(start, nk, bias_lo):
        for h in range(N_HEADS):
            kh = h // group
            qh = q_ref[0, 0, :, h * HEAD_DIM:(h + 1) * HEAD_DIM]
            kw = kvt_ref[0, 0, kh * HEAD_DIM:(kh + 1) * HEAD_DIM, pl.ds(start, nk)]
            s_scr[h, :, :nk] = _dot(qh, kw) + bias_ref[h, :, bias_lo:bias_lo + nk]
        for hp in range(N_HEADS // 2):
            lse_ab, fac_ab = [], []
            for h in (2 * hp, 2 * hp + 1):
                s = s_scr[h, :, :nk]
                m = jnp.max(s, axis=-1, keepdims=True)
                e = jnp.exp(s - m)
                den = jnp.sum(e, axis=-1, keepdims=True)
                e_scr[h, :, :nk] = e.astype(BF16)
                lse = m + jnp.log(den)
                fac = 1.0 / den
                if with_sinks:
                    fac = fac * jax.nn.sigmoid(lse - sink_ref[h])
                lse_ab.append(lse)
                fac_ab.append(fac)
            ps = slice(hp * 2 * HEAD_DIM, (hp + 1) * 2 * HEAD_DIM)
            fac_scr[:, ps] = pair_tile(*fac_ab)
            if not with_sinks:
                l_scr[:, ps] = pair_tile(*lse_ab)
        for hp in range(N_HEADS // 2):
            kh_a, kh_b = (2 * hp) // group, (2 * hp + 1) // group
            v_a = kvt_ref[0, 0, ck + kh_a * HEAD_DIM:ck + (kh_a + 1) * HEAD_DIM, pl.ds(start, nk)]
            if kh_a == kh_b:
                vw = jnp.concatenate([v_a, v_a], axis=0)
            else:
                vw = kvt_ref[0, 0, ck + kh_a * HEAD_DIM:ck + (kh_b + 1) * HEAD_DIM, pl.ds(start, nk)]
            ps = slice(hp * 2 * HEAD_DIM, (hp + 1) * 2 * HEAD_DIM)
            acc = jnp.where(lane_lo, _dot_nt(e_scr[2 * hp, :, :nk], vw), _dot_nt(e_scr[2 * hp + 1, :, :nk], vw))
            f_scr[:, ps] = acc * fac_scr[:, ps]

    if n_blocks == 1:
        run(0, BLOCK, BLOCK)
    else:
        @pl.when(i == 0)
        def _():
            run(0, BLOCK, BLOCK)

        @pl.when(i > 0)
        def _():
            run(pl.multiple_of((i - 1) * BLOCK, BLOCK), 2 * BLOCK, 0)

    rows = None if d == 1 else pl.ds(i * (BLOCK * d) + r, BLOCK, stride=d)
    _store_chunked(o_ref, f_scr[...], rows)
    if not with_sinks:
        _store_chunked(lse_ref, l_scr[...], rows)


def band_attention(q, kvt, bias, sinks, *, name):
    bsz, d, lsub, c = q.shape
    c2 = kvt.shape[2]
    nb = lsub // BLOCK
    with_sinks = sinks is not None
    kern = functools.partial(_band_attn_kernel, n_kv_heads=c2 // (2 * HEAD_DIM), n_blocks=nb, d=d,
                             with_sinks=with_sinks)
    if d == 1:
        o_spec = pl.BlockSpec((1, N_CHUNKS, BLOCK, LANES), lambda b, r, i, *_: (b, 0, i, 0))
    else:
        o_spec = pl.BlockSpec((1, N_CHUNKS, lsub * d, LANES), lambda b, r, i, *_: (b, 0, 0, 0))
    o_shape = jax.ShapeDtypeStruct((bsz, N_CHUNKS, lsub * d, LANES), F32)
    in_specs = [
        pl.BlockSpec((1, 1, BLOCK, c), lambda b, r, i, *_: (b, r, i, 0)),
        pl.BlockSpec((1, 1, c2, lsub), lambda b, r, i, *_: (b, r, 0, 0)),
        pl.BlockSpec(bias.shape, lambda b, r, i, *_: (0, 0, 0), pipeline_mode=pl.Buffered(1)),
    ]
    scratch = [pltpu.VMEM((N_HEADS, BLOCK, 2 * BLOCK), F32), pltpu.VMEM((N_HEADS, BLOCK, 2 * BLOCK), BF16),
               pltpu.VMEM((BLOCK, c), F32), pltpu.VMEM((BLOCK, c), F32)]
    grid = (bsz, d, nb)
    if with_sinks:
        return pl.pallas_call(
            kern,
            grid_spec=pltpu.PrefetchScalarGridSpec(
                num_scalar_prefetch=1, grid=grid, in_specs=in_specs, out_specs=o_spec, scratch_shapes=scratch),
            out_shape=o_shape, compiler_params=_cparams(3), name=name,
        )(sinks, q, kvt, bias)
    return pl.pallas_call(
        kern, grid=grid, in_specs=in_specs, out_specs=[o_spec, o_spec], out_shape=[o_shape, o_shape],
        scratch_shapes=scratch + [pltpu.VMEM((BLOCK, c), F32)],
        compiler_params=_cparams(3), name=name,
    )(q, kvt, bias)


def _window_attn_kernel(*refs, heads_per_step, group, t, lb, n_sb, with_sinks):
    if with_sinks:
        sink_ref, q_ref, cache_ref, new_ref, bias_ref, o_ref, state_ref = refs
    else:
        q_ref, cache_ref, new_ref, bias_ref, o_ref, lse_ref, state_ref = refs
    step = pl.program_id(0)
    hc = pl.program_id(1)
    rows = heads_per_step * HEAD_DIM
    per_tile = LANES // t
    lane = lax.broadcasted_iota(jnp.int32, (rows, LANES), 1)
    keep = lane < LANES - t
    n_tiles = lb // LANES
    for sb in range(n_sb):
        n = step * n_sb + sb
        shift = (LANES - t) - (n % per_tile) * t
        new_k = pltpu.roll(new_ref[0], shift, 1)
        new_v = pltpu.roll(new_ref[1], shift, 1)
        new_kb, new_vb = new_k.astype(BF16), new_v.astype(BF16)
        scores = []
        for kh in range(heads_per_step):
            rr = slice(kh * HEAD_DIM, (kh + 1) * HEAD_DIM)
            kt = jnp.concatenate([cache_ref[sb, 0, rr, :].astype(BF16), new_kb[rr]], axis=1)
            for g in range(group):
                hl = kh * group + g
                qh = q_ref[sb, :, hl * HEAD_DIM:(hl + 1) * HEAD_DIM].astype(BF16)
                scores.append(_dot(qh, kt) + bias_ref[hl])
        probs, lses = [], []
        for s in scores:
            m = jnp.max(s, axis=-1, keepdims=True)
            e = jnp.exp(s - m)
            den = jnp.sum(e, axis=-1, keepdims=True)
            probs.append((e / den).astype(BF16))
            lses.append(m + jnp.log(den))
        for kh in range(heads_per_step):
            rr = slice(kh * HEAD_DIM, (kh + 1) * HEAD_DIM)
            vt = jnp.concatenate([cache_ref[sb, 1, rr, :].astype(BF16), new_vb[rr]], axis=1)
            for g in range(group):
                hl = kh * group + g
                hs = slice(hl * HEAD_DIM, (hl + 1) * HEAD_DIM)
                out = _dot_nt(probs[hl], vt)
                if with_sinks:
                    h_abs = hc * (heads_per_step * group) + hl
                    out = out * jax.nn.sigmoid(lses[hl] - sink_ref[h_abs])
                else:
                    lse_ref[sb, :, hs] = jnp.broadcast_to(lses[hl], (t, HEAD_DIM))
                o_ref[sb, :, hs] = out
        for kv, new in ((0, new_k), (1, new_v)):
            rolled = pltpu.roll(cache_ref[sb, kv, :, 0:LANES], LANES - t, 1)
            for j in range(n_tiles):
                if j + 1 < n_tiles:
                    nxt = pltpu.roll(cache_ref[sb, kv, :, (j + 1) * LANES:(j + 2) * LANES], LANES - t, 1)
                else:
                    nxt = new
                state_ref[sb, kv, :, j * LANES:(j + 1) * LANES] = jnp.where(keep, rolled, nxt)
                rolled = nxt


def window_attention(q, cache_t, new_t, bias, sinks, *, heads_per_step, samples_per_step, name):
    n_s, t, cq = q.shape
    _, _, ck, lb = cache_t.shape
    n_kv = ck // HEAD_DIM
    group = (cq // HEAD_DIM) // n_kv
    n_hc = n_kv // heads_per_step
    rows = heads_per_step * HEAD_DIM
    cq_step = rows * group
    n_sb = samples_per_step
    steps_per_tile = (LANES // t) // n_sb
    with_sinks = sinks is not None
    kern = functools.partial(_window_attn_kernel, heads_per_step=heads_per_step, group=group, t=t, lb=lb,
                             n_sb=n_sb, with_sinks=with_sinks)
    o_spec = pl.BlockSpec((n_sb, t, cq_step), lambda n, hc, *_: (n, 0, hc))
    o_shape = jax.ShapeDtypeStruct((n_s, t, cq), F32)
    st_spec = pl.BlockSpec((n_sb, 2, rows, lb), lambda n, hc, *_: (n, 0, hc, 0))
    st_shape = jax.ShapeDtypeStruct(cache_t.shape, F32)
    in_specs = [
        pl.BlockSpec((n_sb, t, cq_step), lambda n, hc, *_: (n, 0, hc)),
        pl.BlockSpec((n_sb, 2, rows, lb), lambda n, hc, *_: (n, 0, hc, 0)),
        pl.BlockSpec((2, rows, LANES), lambda n, hc, *_: (0, hc, n // steps_per_tile)),
        pl.BlockSpec((heads_per_step * group, t, lb + LANES), lambda n, hc, *_: (hc, 0, 0)),
    ]
    grid = (n_s // n_sb, n_hc)
    if with_sinks:
        return pl.pallas_call(
            kern,
            grid_spec=pltpu.PrefetchScalarGridSpec(
                num_scalar_prefetch=1, grid=grid, in_specs=in_specs, out_specs=[o_spec, st_spec]),
            out_shape=[o_shape, st_shape], compiler_params=_cparams(2), name=name,
        )(sinks, q, cache_t, new_t, bias)
    return pl.pallas_call(
        kern, grid=grid, in_specs=in_specs, out_specs=[o_spec, o_spec, st_spec],
        out_shape=[o_shape, o_shape, st_shape], compiler_params=_cparams(2), name=name,
    )(q, cache_t, new_t, bias)


def _mix_out_kernel(*refs, n_groups, chunked):
    o_refs = refs[:n_groups]
    lse_refs = refs[n_groups:2 * n_groups] if n_groups > 1 else ()
    x_ref, wo_ref, g_ref, b_ref, out_ref = refs[len(o_refs) + len(lse_refs):]
    load = _load_chunked if chunked else (lambda r: r[...])
    if n_groups == 1:
        mixed = load(o_refs[0])
    else:
        lses = [load(r) for r in lse_refs]
        m = functools.reduce(jnp.maximum, lses)
        es = [jnp.exp(l - m) for l in lses]
        tot = functools.reduce(lambda a, b: a + b, es)
        mixed = functools.reduce(lambda a, b: a + b, [(e / tot) * load(r) for e, r in zip(es, o_refs)])
    y = _dot(mixed.astype(BF16), wo_ref[...])
    out_ref[...] = _layer_norm(ALPHA * x_ref[...] + y, g_ref[...], b_ref[...])


def mix_out(outs, lses, x, wo, gain, bias, tm=512):
    rows, d = x.shape
    tm = min(tm, rows)
    ng = len(outs)
    row_spec = pl.BlockSpec((tm, d), lambda i: (i, 0))
    chunked = outs[0].ndim == 4
    if chunked:
        per_seq = outs[0].shape[2] // tm
        mix_spec = pl.BlockSpec((1, N_CHUNKS, tm, LANES), lambda i: (i // per_seq, 0, i % per_seq, 0))
    else:
        mix_spec = row_spec
    return pl.pallas_call(
        functools.partial(_mix_out_kernel, n_groups=ng, chunked=chunked),
        grid=(rows // tm,),
        in_specs=[mix_spec] * (ng + len(lses)) + [row_spec, _resident(wo.shape), _resident((1, d)),
                                                   _resident((1, d))],
        out_specs=row_spec,
        out_shape=jax.ShapeDtypeStruct((rows, d), F32),
        compiler_params=_cparams(1),
        name="mix_out",
    )(*outs, *lses, x, wo, gain, bias)


WINDOW_BLOCK_BYTES = 4 * 1024 * 1024
MAX_SAMPLES_PER_STEP = 8


def _window_tiling(n_kv, lb):
    head_bytes = 2 * HEAD_DIM * lb * 4
    heads = max(1, min(n_kv, WINDOW_BLOCK_BYTES // head_bytes))
    samples = 1
    if heads == n_kv:
        samples = max(1, min(MAX_SAMPLES_PER_STEP, WINDOW_BLOCK_BYTES // (head_bytes * n_kv)))
    return heads, samples


def _to_pos_minor(cache):
    n, lb, _, h, dh = cache.shape
    return jnp.transpose(cache, (0, 2, 3, 4, 1)).reshape(n, 2, h * dh, lb)


def _from_pos_minor(state_t, n_heads):
    n, _, _, lb = state_t.shape
    return jnp.transpose(state_t.reshape(n, 2, n_heads, HEAD_DIM, lb), (0, 4, 1, 2, 3))[None]


def kernel(x_prompt, x_sample, cache_a_kv, cache_b1_kv, cache_b2_kv, cache_b3_kv, rel_bias_table, ln_gain, ln_bias, ffn_w_gate, ffn_w_up, ffn_w_down, attn_a_w_qkv, attn_a_w_o, attn_a_sinks, attn_b_w_qkv, attn_b_w_o):
    bsz, seq, dm = x_prompt.shape
    n_s, t, _ = x_sample.shape
    hd = N_HEADS * HEAD_DIM
    ck_a = N_KV_HEADS_A * HEAD_DIM
    xp = x_prompt.reshape(bsz * seq, dm)
    xs = x_sample.reshape(n_s * t, dm)
    table = rel_bias_table.astype(F32)

    def ln_params(i, j):
        return ln_gain[i, j].reshape(1, dm).astype(F32), ln_bias[i, j].reshape(1, dm).astype(F32)

    def half_ffn(x, i, f, j, chunked_seq=None):
        g, b = ln_params(i, j)
        return ffn_ln(x, ffn_w_gate[i, f].astype(BF16), ffn_w_up[i, f].astype(BF16),
                      ffn_w_down[i, f].astype(BF16), g, b, chunked_seq=chunked_seq)

    def split_qkv(w, q_cols, kv_cols):
        wq = (w[:, q_cols[0]:q_cols[1]] * SCALE).astype(BF16)
        wkvt = w[:, kv_cols[0]:kv_cols[1]].T.astype(BF16)
        return wq, wkvt

    def sample_mixer(xs, cache, wq, wkvt, n_kv, dilation, units, sinks, name):
        xs3 = xs.reshape(1, n_s * t, dm)
        q, new_t = project(xs3, wq, wkvt, tl=n_s * t, l_off=0, l_len=n_s * t, kv_dtype=F32, q_dtype=F32,
                           name="proj_" + name)
        lb = cache.shape[1]
        heads, samples = _window_tiling(n_kv, lb)
        return window_attention(
            q.reshape(n_s, t, hd), _to_pos_minor(cache), new_t.reshape(2, n_kv * HEAD_DIM, n_s * t),
            bias_lookup(table, _sample_bias_idx(dilation, units, lb, t)), sinks,
            heads_per_step=heads, samples_per_step=samples, name="win_" + name)

    li = 0
    xp = half_ffn(xp, 0, 0, 0)
    xs = half_ffn(xs, 0, 0, 0)
    wq, wkvt = split_qkv(attn_a_w_qkv[li], (0, hd), (hd, hd + 2 * ck_a))
    wo = attn_a_w_o[li].astype(BF16)
    sinks = attn_a_sinks[li].astype(F32)
    g1, b1 = ln_params(0, 1)
    keep = min(WINDOW_A, seq)

    xp3 = xp.reshape(bsz, seq, dm)
    q, kvt = project(xp3, wq, wkvt, tl=512, l_off=0, l_len=seq, kv_dtype=BF16, name="proj_a")
    (st,) = project(xp3, None, wkvt, tl=keep, l_off=seq - keep, l_len=keep, kv_dtype=F32, name="state_a")
    state_a_p = _from_pos_minor(st.reshape(bsz, 2, ck_a, keep), N_KV_HEADS_A)
    o = band_attention(q, kvt, bias_lookup(table, _prompt_bias_idx(1, WINDOW_A)), sinks, name="attn_a")
    xp = mix_out([o], [], xp, wo, g1, b1)

    o, state_a_s = sample_mixer(xs, cache_a_kv[li], wq, wkvt, N_KV_HEADS_A, 1, WINDOW_A, sinks, "a_s")
    state_a_s = _from_pos_minor(state_a_s, N_KV_HEADS_A)
    xs = mix_out([o.reshape(n_s * t, hd)], [], xs, wo, g1, b1)

    xp = half_ffn(xp, 0, 1, 2)
    xs = half_ffn(xs, 0, 1, 2)

    xp, xp_chunked = half_ffn(xp, 1, 0, 0, chunked_seq=seq)
    xs = half_ffn(xs, 1, 0, 0)
    w_qkv = attn_b_w_qkv[li]
    wo = attn_b_w_o[li].astype(BF16)
    g1, b1 = ln_params(1, 1)
    caches = (cache_b1_kv[li], cache_b2_kv[li], cache_b3_kv[li])
    xp3 = xp.reshape(bsz, seq, dm)
    outs_p, lses_p, outs_s, lses_s, states_p, states_s = [], [], [], [], [], []
    for gi, (w, d) in enumerate(DILATED_GROUPS):
        base = gi * 3 * hd
        wq, wkvt = split_qkv(w_qkv, (base, base + hd), (base + hd, base + 3 * hd))
        units = w // d
        if d == 1:
            q, kvt = project(xp3, wq, wkvt, tl=512, l_off=0, l_len=seq, kv_dtype=BF16, name=f"proj_b{gi}")
        else:
            q, kvt = project_strided(xp_chunked, wq, wkvt, d=d, rows_per_step=512, name=f"proj_b{gi}")
        keep = min(w, seq)
        (st,) = project(xp3, None, wkvt, tl=min(keep, 512), l_off=seq - keep, l_len=keep, kv_dtype=F32,
                        name=f"state_b{gi}")
        states_p.append(_from_pos_minor(st.reshape(bsz, 2, hd, keep), N_HEADS))
        o, lse = band_attention(q, kvt, bias_lookup(table, _prompt_bias_idx(d, units)), None, name=f"attn_b{gi}")
        outs_p.append(o)
        lses_p.append(lse)
        o, lse, st = sample_mixer(xs, caches[gi], wq, wkvt, N_HEADS, d, units, None, f"b{gi}_s")
        outs_s.append(o.reshape(n_s * t, hd))
        lses_s.append(lse.reshape(n_s * t, hd))
        states_s.append(_from_pos_minor(st, N_HEADS))
    xp = mix_out(outs_p, lses_p, xp, wo, g1, b1)
    xs = mix_out(outs_s, lses_s, xs, wo, g1, b1)
    xp = half_ffn(xp, 1, 1, 2)
    xs = half_ffn(xs, 1, 1, 2)

    return (xp.reshape(bsz, seq, dm), xs.reshape(n_s, t, dm),
            state_a_p, states_p[0], states_p[1], states_p[2],
            state_a_s, states_s[0], states_s[1], states_s[2])
```

```python
import functools
import math

import jax
import jax.numpy as jnp
from jax import lax
from jax.experimental import pallas as pl
from jax.experimental.pallas import tpu as pltpu

D_MODEL = 1024
HEAD_DIM = 64
N_HEADS = 16
N_KV_HEADS_A = 4
WINDOW_A = 128
DILATED_GROUPS = ((128, 1), (512, 4), (2048, 16))
BLOCK = 128
NUM_BUCKETS = 32
MAX_DISTANCE = 2048
DEPTH = 2
ALPHA = (2 * DEPTH) ** 0.25
LN_EPS = 1e-5
NEG_INF = -1e30
SCALE = HEAD_DIM ** -0.5

LANES = 128
N_CHUNKS = D_MODEL // LANES
V7X_VMEM_BYTES = 64 * 1024 * 1024
VMEM_LIMIT = 56 * 1024 * 1024
FFN_WIN_VMEM_LIMIT = 60 * 1024 * 1024

F32 = jnp.float32
BF16 = jnp.bfloat16


def _cparams(n_axes, vmem=VMEM_LIMIT):
    return pltpu.CompilerParams(dimension_semantics=("arbitrary",) * n_axes, vmem_limit_bytes=vmem)


def _resident(shape):
    nd = len(shape)
    return pl.BlockSpec(shape, lambda *_: (0,) * nd, pipeline_mode=pl.Buffered(1))


def _dot(a, b):
    return jnp.dot(a, b, preferred_element_type=F32)


def _dot_nt(a, b):
    return lax.dot_general(a, b, (((1,), (1,)), ((), ())), preferred_element_type=F32)


def _layer_norm(y, g, b):
    mu = jnp.mean(y, axis=-1, keepdims=True)
    yc = y - mu
    var = jnp.mean(yc * yc, axis=-1, keepdims=True)
    return yc * lax.rsqrt(var + LN_EPS) * g + b


FFN_SPLITS = (768, 768, 640, 640)


def _ffn_ln_kernel(*refs, splits, chunked, win):
    x_ref, wg_ref, wu_ref, wd_ref, g_ref, b_ref = refs[:6]
    pos = 6
    if win is not None:
        wq_ref, wcache_ref, wnew_ref, wbias_ref = refs[pos:pos + 4]
        pos += 4 + (1 if win["aliased"] else 0)
    o_ref = refs[pos]
    pos += 1
    if chunked:
        chunked_ref = refs[pos]
        pos += 1
    if win is not None:
        wo_ref, wlse_ref, wstate_ref = refs[pos:pos + 3]
        pos += 3
    xb_scr, acc_scr = refs[pos:pos + 2]
    i = pl.program_id(0)
    j = pl.program_id(1)
    last = len(splits) - 1
    off = 0
    for k, size in enumerate(splits):
        sl = slice(off, off + size)
        off += size

        @pl.when(j == k)
        def _(k=k, sl=sl):
            if k == 0:
                xb = x_ref[...].astype(BF16)
                xb_scr[...] = xb
            else:
                xb = xb_scr[...]
            gate = _dot(xb, wg_ref[:, sl])
            up = _dot(xb, wu_ref[:, sl])
            hid = gate * jax.nn.sigmoid(gate) * up
            part = _dot(hid.astype(BF16), wd_ref[sl, :])
            if k == 0:
                acc_scr[...] = part
            elif k < last:
                acc_scr[...] += part
            else:
                y = _layer_norm(ALPHA * x_ref[...] + 0.5 * (acc_scr[...] + part), g_ref[...], b_ref[...])
                o_ref[...] = y
                if chunked:
                    _store_chunked(chunked_ref, y)
            if win is not None:
                unit = i * len(splits) + j
                _window_unit(wq_ref, wcache_ref, wnew_ref, wbias_ref, wo_ref, wlse_ref, wstate_ref, None,
                             sb=0, n=win["n0"] + unit // win["n_hc"], hc=unit % win["n_hc"],
                             heads_per_step=win["heads_per_step"], group=win["group"], t=win["t"], lb=win["lb"])


def _store_chunked(ref, y, rows=None):
    for c in range(N_CHUNKS):
        if rows is None:
            ref[0, c] = y[:, c * LANES:(c + 1) * LANES]
        else:
            ref[0, c, rows, :] = y[:, c * LANES:(c + 1) * LANES]


def _load_chunked(ref, rows=None):
    if rows is None:
        return jnp.concatenate([ref[0, c] for c in range(N_CHUNKS)], axis=1)
    return jnp.concatenate([ref[0, c, rows, :] for c in range(N_CHUNKS)], axis=1)


def ffn_ln(x, wg, wu, wd, gain, bias, tm=512, chunked_seq=None, win=None):
    rows, d = x.shape
    d_ff = wg.shape[1]
    tm = min(tm, rows)
    splits = FFN_SPLITS
    assert sum(splits) == d_ff
    nf = len(splits)
    in_specs = [
        pl.BlockSpec((tm, d), lambda i, j: (i, 0)),
        _resident((d, d_ff)), _resident((d, d_ff)), _resident((d_ff, d)),
        _resident((1, d)), _resident((1, d)),
    ]
    args = [x, wg, wu, wd, gain, bias]
    out_specs = [pl.BlockSpec((tm, d), lambda i, j: (i, 0))]
    out_shape = [jax.ShapeDtypeStruct((rows, d), F32)]
    if chunked_seq is not None:
        per_seq = chunked_seq // tm
        out_specs.append(pl.BlockSpec((1, N_CHUNKS, tm, LANES), lambda i, j: (i // per_seq, 0, i % per_seq, 0)))
        out_shape.append(jax.ShapeDtypeStruct((rows // chunked_seq, N_CHUNKS, chunked_seq, LANES), F32))
    aliases = {}
    win_static = None
    if win is not None:
        assert win["n_count"] * win["n_hc"] == (rows // tm) * nf
        unit_of = lambda i, j: i * nf + j
        w_in, w_out, w_shape = window_specs(win, unit_of)
        in_specs += w_in
        args += [win["q"], win["cache_t"], win["new_t"], win["bias"]]
        if win["prev_state"] is not None:
            in_specs.append(pl.BlockSpec(memory_space=pl.ANY))
            args.append(win["prev_state"])
            aliases = {len(args) - 1: len(out_specs) + 2}
        out_specs += w_out
        out_shape += w_shape
        win_static = dict(n0=win["n0"], n_hc=win["n_hc"], heads_per_step=win["heads_per_step"],
                          group=win["group"], t=win["t"], lb=win["lb"], aliased=win["prev_state"] is not None)
    res = pl.pallas_call(
        functools.partial(_ffn_ln_kernel, splits=splits, chunked=chunked_seq is not None, win=win_static),
        grid=(rows // tm, nf),
        in_specs=in_specs,
        out_specs=out_specs,
        out_shape=out_shape,
        scratch_shapes=[pltpu.VMEM((tm, d), BF16), pltpu.VMEM((tm, d), F32)],
        input_output_aliases=aliases,
        compiler_params=_cparams(2, FFN_WIN_VMEM_LIMIT if win is not None else VMEM_LIMIT),
        name="ffn_ln_win" if win is not None else "ffn_ln",
    )(*args)
    return res if len(res) > 1 else res[0]


def _proj_kernel(*refs, with_q, d, lsub, res_per_step):
    if with_q:
        x_ref, wq_ref, wkvt_ref, q_ref, kvt_ref = refs
    else:
        x_ref, wkvt_ref, kvt_ref = refs
    if d == 1:
        xb = x_ref[0].astype(BF16)
    else:
        r0 = pl.program_id(1) * res_per_step
        xb = jnp.concatenate(
            [_load_chunked(x_ref, pl.ds(r0 + rr, lsub, stride=d)) for rr in range(res_per_step)],
            axis=0).astype(BF16)
    if with_q:
        q = _dot(xb, wq_ref[...]).astype(q_ref.dtype)
        for rr in range(res_per_step):
            q_ref[0, rr] = q[rr * lsub:(rr + 1) * lsub]
    kvt = _dot_nt(wkvt_ref[...], xb).astype(kvt_ref.dtype)
    for rr in range(res_per_step):
        kvt_ref[0, rr] = kvt[:, rr * lsub:(rr + 1) * lsub]


def project(x3, wq, wkvt, *, tl, l_off, l_len, kv_dtype, name, q_dtype=BF16):
    bsz, s, dm = x3.shape
    c2 = wkvt.shape[0]
    nl = l_len // tl
    off = l_off // tl
    with_q = wq is not None
    in_specs = [pl.BlockSpec((1, tl, dm), lambda b, j: (b, j + off, 0))]
    args = [x3]
    out_specs, out_shape = [], []
    if with_q:
        cq = wq.shape[1]
        in_specs.append(_resident(wq.shape))
        args.append(wq)
        out_specs.append(pl.BlockSpec((1, 1, tl, cq), lambda b, j: (b, 0, j, 0)))
        out_shape.append(jax.ShapeDtypeStruct((bsz, 1, l_len, cq), q_dtype))
    in_specs.append(_resident(wkvt.shape))
    args.append(wkvt)
    out_specs.append(pl.BlockSpec((1, 1, c2, tl), lambda b, j: (b, 0, 0, j)))
    out_shape.append(jax.ShapeDtypeStruct((bsz, 1, c2, l_len), kv_dtype))
    return pl.pallas_call(
        functools.partial(_proj_kernel, with_q=with_q, d=1, lsub=tl, res_per_step=1),
        grid=(bsz, nl),
        in_specs=in_specs, out_specs=out_specs, out_shape=out_shape,
        compiler_params=_cparams(2),
        name=name,
    )(*args)


def project_strided(xc, wq, wkvt, *, d, rows_per_step, name):
    bsz, _, s, _ = xc.shape
    lsub = s // d
    rps = max(1, rows_per_step // lsub)
    c2, cq = wkvt.shape[0], wq.shape[1]
    return pl.pallas_call(
        functools.partial(_proj_kernel, with_q=True, d=d, lsub=lsub, res_per_step=rps),
        grid=(bsz, d // rps),
        in_specs=[pl.BlockSpec((1, N_CHUNKS, s, LANES), lambda b, j: (b, 0, 0, 0)),
                  _resident(wq.shape), _resident(wkvt.shape)],
        out_specs=[pl.BlockSpec((1, rps, lsub, cq), lambda b, j: (b, j, 0, 0)),
                   pl.BlockSpec((1, rps, c2, lsub), lambda b, j: (b, j, 0, 0))],
        out_shape=[jax.ShapeDtypeStruct((bsz, d, lsub, cq), BF16),
                   jax.ShapeDtypeStruct((bsz, d, c2, lsub), BF16)],
        compiler_params=_cparams(2),
        name=name,
    )(xc, wq, wkvt)


def _bias_kernel(table_ref, idx_ref, o_ref):
    h = pl.program_id(0)
    idx = idx_ref[...]
    acc = jnp.full(idx.shape, NEG_INF, F32)
    for b in range(NUM_BUCKETS):
        acc = jnp.where(idx == b, table_ref[b * N_HEADS + h], acc)
    o_ref[0] = acc


def bias_lookup(table, idx):
    r, c = idx.shape
    return pl.pallas_call(
        _bias_kernel,
        grid_spec=pltpu.PrefetchScalarGridSpec(
            num_scalar_prefetch=1,
            grid=(N_HEADS,),
            in_specs=[pl.BlockSpec((r, c), lambda h, t: (0, 0))],
            out_specs=pl.BlockSpec((1, r, c), lambda h, t: (h, 0, 0)),
        ),
        out_shape=jax.ShapeDtypeStruct((N_HEADS, r, c), F32),
        name="bias_lookup",
    )(table.reshape(-1), idx)


def _t5_bucket(dist):
    max_exact = NUM_BUCKETS // 2
    d = jnp.maximum(dist.astype(F32), 1.0)
    large = max_exact + (jnp.log(d / max_exact) / math.log(MAX_DISTANCE / max_exact)
                         * (NUM_BUCKETS - max_exact)).astype(jnp.int32)
    large = jnp.minimum(large, NUM_BUCKETS - 1)
    return jnp.where(dist < max_exact, dist, large)


def _prompt_bias_idx(dilation, window_units):
    qi = jnp.arange(BLOCK)[:, None]
    ki = jnp.arange(2 * BLOCK)[None, :]
    rel = qi + BLOCK - ki
    band = (rel >= 0) & (rel <= window_units)
    return jnp.where(band, _t5_bucket(jnp.maximum(rel, 0) * dilation), -1).astype(jnp.int32)


def _sample_bias_idx(dilation, window_units, lb, t):
    tau = jnp.arange(t)[:, None]
    pos = jnp.arange(lb)[None, :]
    dist = lb + tau - pos
    ok = (dist % dilation == 0) & (dist // dilation <= window_units)
    idx_c = jnp.where(ok, _t5_bucket(dist), -1).astype(jnp.int32)
    nu = jnp.arange(LANES)[None, :] - (LANES - t)
    dist_n = tau - nu
    ok_n = (nu >= 0) & (dist_n >= 0) & (dist_n % dilation == 0) & (dist_n // dilation <= window_units)
    idx_n = jnp.where(ok_n, _t5_bucket(jnp.maximum(dist_n, 0)), -1).astype(jnp.int32)
    return jnp.concatenate([idx_c, idx_n], axis=1)


def _band_attn_kernel(*refs, n_kv_heads, n_blocks, d, with_sinks):
    if with_sinks:
        sink_ref, q_ref, kvt_ref, bias_ref, o_ref, s_scr, e_scr, f_scr, fac_scr = refs
    else:
        q_ref, kvt_ref, bias_ref, o_ref, lse_ref, s_scr, e_scr, f_scr, fac_scr, l_scr = refs
    r = pl.program_id(1)
    i = pl.program_id(2)
    ck = n_kv_heads * HEAD_DIM
    group = N_HEADS // n_kv_heads

    lane_lo = lax.broadcasted_iota(jnp.int32, (BLOCK, 2 * HEAD_DIM), 1) < HEAD_DIM

    def pair_tile(a, b):
        return jnp.where(lane_lo, a, b)

    def run(start, nk, bias_lo):
        for h in range(N_HEADS):
            kh = h // group
            qh = q_ref[0, 0, :, h * HEAD_DIM:(h + 1) * HEAD_DIM]
            kw = kvt_ref[0, 0, kh * HEAD_DIM:(kh + 1) * HEAD_DIM, pl.ds(start, nk)]
            s_scr[h, :, :nk] = _dot(qh, kw) + bias_ref[h, :, bias_lo:bias_lo + nk]
        for hp in range(N_HEADS // 2):
            lse_ab, fac_ab = [], []
            for h in (2 * hp, 2 * hp + 1):
                s = s_scr[h, :, :nk]
                m = jnp.max(s, axis=-1, keepdims=True)
                e = jnp.exp(s - m)
                den = jnp.sum(e, axis=-1, keepdims=True)
                e_scr[h, :, :nk] = e.astype(BF16)
                lse = m + jnp.log(den)
                fac = 1.0 / den
                if with_sinks:
                    fac = fac * jax.nn.sigmoid(lse - sink_ref[h])
                lse_ab.append(lse)
                fac_ab.append(fac)
            ps = slice(hp * 2 * HEAD_DIM, (hp + 1) * 2 * HEAD_DIM)
            fac_scr[:, ps] = pair_tile(*fac_ab)
            if not with_sinks:
                l_scr[:, ps] = pair_tile(*lse_ab)
        for hp in range(N_HEADS // 2):
            kh_a, kh_b = (2 * hp) // group, (2 * hp + 1) // group
            v_a = kvt_ref[0, 0, ck + kh_a * HEAD_DIM:ck + (kh_a + 1) * HEAD_DIM, pl.ds(start, nk)]
            if kh_a == kh_b:
                vw = jnp.concatenate([v_a, v_a], axis=0)
            else:
                vw = kvt_ref[0, 0, ck + kh_a * HEAD_DIM:ck + (kh_b + 1) * HEAD_DIM, pl.ds(start, nk)]
            ps = slice(hp * 2 * HEAD_DIM, (hp + 1) * 2 * HEAD_DIM)
            acc = jnp.where(lane_lo, _dot_nt(e_scr[2 * hp, :, :nk], vw), _dot_nt(e_scr[2 * hp + 1, :, :nk], vw))
            f_scr[:, ps] = acc * fac_scr[:, ps]

    if n_blocks == 1:
        run(0, BLOCK, BLOCK)
    else:
        @pl.when(i == 0)
        def _():
            run(0, BLOCK, BLOCK)

        @pl.when(i > 0)
        def _():
            run(pl.multiple_of((i - 1) * BLOCK, BLOCK), 2 * BLOCK, 0)

    rows = None if d == 1 else pl.ds(i * (BLOCK * d) + r, BLOCK, stride=d)
    _store_chunked(o_ref, f_scr[...], rows)
    if not with_sinks:
        _store_chunked(lse_ref, l_scr[...], rows)


def band_attention(q, kvt, bias, sinks, *, name):
    bsz, d, lsub, c = q.shape
    c2 = kvt.shape[2]
    nb = lsub // BLOCK
    with_sinks = sinks is not None
    kern = functools.partial(_band_attn_kernel, n_kv_heads=c2 // (2 * HEAD_DIM), n_blocks=nb, d=d,
                             with_sinks=with_sinks)
    if d == 1:
        o_spec = pl.BlockSpec((1, N_CHUNKS, BLOCK, LANES), lambda b, r, i, *_: (b, 0, i, 0))
    else:
        o_spec = pl.BlockSpec((1, N_CHUNKS, lsub * d, LANES), lambda b, r, i, *_: (b, 0, 0, 0))
    o_shape = jax.ShapeDtypeStruct((bsz, N_CHUNKS, lsub * d, LANES), F32)
    in_specs = [
        pl.BlockSpec((1, 1, BLOCK, c), lambda b, r, i, *_: (b, r, i, 0)),
        pl.BlockSpec((1, 1, c2, lsub), lambda b, r, i, *_: (b, r, 0, 0)),
        pl.BlockSpec(bias.shape, lambda b, r, i, *_: (0, 0, 0), pipeline_mode=pl.Buffered(1)),
    ]
    scratch = [pltpu.VMEM((N_HEADS, BLOCK, 2 * BLOCK), F32), pltpu.VMEM((N_HEADS, BLOCK, 2 * BLOCK), BF16),
               pltpu.VMEM((BLOCK, c), F32), pltpu.VMEM((BLOCK, c), F32)]
    grid = (bsz, d, nb)
    if with_sinks:
        return pl.pallas_call(
            kern,
            grid_spec=pltpu.PrefetchScalarGridSpec(
                num_scalar_prefetch=1, grid=grid, in_specs=in_specs, out_specs=o_spec, scratch_shapes=scratch),
            out_shape=o_shape, compiler_params=_cparams(3), name=name,
        )(sinks, q, kvt, bias)
    return pl.pallas_call(
        kern, grid=grid, in_specs=in_specs, out_specs=[o_spec, o_spec], out_shape=[o_shape, o_shape],
        scratch_shapes=scratch + [pltpu.VMEM((BLOCK, c), F32)],
        compiler_params=_cparams(3), name=name,
    )(q, kvt, bias)


def _window_unit(q_ref, cache_ref, new_ref, bias_ref, o_ref, lse_ref, state_ref, sink_ref, *,
                 sb, n, hc, heads_per_step, group, t, lb):
    rows = heads_per_step * HEAD_DIM
    per_tile = LANES // t
    lane = lax.broadcasted_iota(jnp.int32, (rows, LANES), 1)
    keep = lane < LANES - t
    n_tiles = lb // LANES
    shift = (LANES - t) - (n % per_tile) * t
    new_k = pltpu.roll(new_ref[0], shift, 1)
    new_v = pltpu.roll(new_ref[1], shift, 1)
    new_kb, new_vb = new_k.astype(BF16), new_v.astype(BF16)
    scores = []
    for kh in range(heads_per_step):
        rr = slice(kh * HEAD_DIM, (kh + 1) * HEAD_DIM)
        kt = jnp.concatenate([cache_ref[sb, 0, rr, :].astype(BF16), new_kb[rr]], axis=1)
        for g in range(group):
            hl = kh * group + g
            qh = q_ref[sb, :, hl * HEAD_DIM:(hl + 1) * HEAD_DIM].astype(BF16)
            scores.append(_dot(qh, kt) + bias_ref[hl])
    probs, lses = [], []
    for s in scores:
        m = jnp.max(s, axis=-1, keepdims=True)
        e = jnp.exp(s - m)
        den = jnp.sum(e, axis=-1, keepdims=True)
        probs.append((e / den).astype(BF16))
        lses.append(m + jnp.log(den))
    for kh in range(heads_per_step):
        rr = slice(kh * HEAD_DIM, (kh + 1) * HEAD_DIM)
        vt = jnp.concatenate([cache_ref[sb, 1, rr, :].astype(BF16), new_vb[rr]], axis=1)
        for g in range(group):
            hl = kh * group + g
            hs = slice(hl * HEAD_DIM, (hl + 1) * HEAD_DIM)
            out = _dot_nt(probs[hl], vt)
            if sink_ref is not None:
                h_abs = hc * (heads_per_step * group) + hl
                out = out * jax.nn.sigmoid(lses[hl] - sink_ref[h_abs])
            else:
                lse_ref[sb, :, hs] = jnp.broadcast_to(lses[hl], (t, HEAD_DIM))
            o_ref[sb, :, hs] = out
    for kv, new in ((0, new_k), (1, new_v)):
        rolled = pltpu.roll(cache_ref[sb, kv, :, 0:LANES], LANES - t, 1)
        for j in range(n_tiles):
            if j + 1 < n_tiles:
                nxt = pltpu.roll(cache_ref[sb, kv, :, (j + 1) * LANES:(j + 2) * LANES], LANES - t, 1)
            else:
                nxt = new
            state_ref[sb, kv, :, j * LANES:(j + 1) * LANES] = jnp.where(keep, rolled, nxt)
            rolled = nxt


def _window_attn_kernel(*refs, heads_per_step, group, t, lb, n_sb, with_sinks):
    if with_sinks:
        sink_ref, q_ref, cache_ref, new_ref, bias_ref, o_ref, state_ref = refs
        lse_ref = None
    else:
        q_ref, cache_ref, new_ref, bias_ref, o_ref, lse_ref, state_ref = refs
        sink_ref = None
    for sb in range(n_sb):
        _window_unit(q_ref, cache_ref, new_ref, bias_ref, o_ref, lse_ref, state_ref, sink_ref,
                     sb=sb, n=pl.program_id(0) * n_sb + sb, hc=pl.program_id(1),
                     heads_per_step=heads_per_step, group=group, t=t, lb=lb)


def window_work(q, cache_t, new_t, bias, heads_per_step):
    n_s, t, cq = q.shape
    _, _, ck, lb = cache_t.shape
    n_kv = ck // HEAD_DIM
    return dict(q=q, cache_t=cache_t, new_t=new_t, bias=bias, t=t, lb=lb, n_total=n_s,
                heads_per_step=heads_per_step, group=(cq // HEAD_DIM) // n_kv, n_hc=n_kv // heads_per_step,
                n0=0, n_count=n_s, prev_state=None)


def window_specs(win, unit_of):
    t, lb, n_hc, n0 = win["t"], win["lb"], win["n_hc"], win["n0"]
    rows = win["heads_per_step"] * HEAD_DIM
    cq_step = rows * win["group"]
    per_tile = LANES // t

    def samp(*idx):
        return n0 + unit_of(*idx) // n_hc

    def chunk(*idx):
        return unit_of(*idx) % n_hc

    in_specs = [
        pl.BlockSpec((1, t, cq_step), lambda *idx: (samp(*idx), 0, chunk(*idx))),
        pl.BlockSpec((1, 2, rows, lb), lambda *idx: (samp(*idx), 0, chunk(*idx), 0)),
        pl.BlockSpec((2, rows, LANES), lambda *idx: (0, chunk(*idx), samp(*idx) // per_tile)),
        pl.BlockSpec((win["heads_per_step"] * win["group"], t, lb + LANES), lambda *idx: (chunk(*idx), 0, 0)),
    ]
    part_spec = pl.BlockSpec((1, t, cq_step), lambda *idx: (samp(*idx) - n0, 0, chunk(*idx)))
    part_shape = jax.ShapeDtypeStruct((win["n_count"], t, cq_step * n_hc), F32)
    out_specs = [part_spec, part_spec, pl.BlockSpec((1, 2, rows, lb), lambda *idx: (samp(*idx), 0, chunk(*idx), 0))]
    out_shape = [part_shape, part_shape, jax.ShapeDtypeStruct(win["cache_t"].shape, F32)]
    return in_specs, out_specs, out_shape


def window_attention(q, cache_t, new_t, bias, sinks, *, heads_per_step, samples_per_step, name):
    n_s, t, cq = q.shape
    _, _, ck, lb = cache_t.shape
    n_kv = ck // HEAD_DIM
    group = (cq // HEAD_DIM) // n_kv
    n_hc = n_kv // heads_per_step
    rows = heads_per_step * HEAD_DIM
    cq_step = rows * group
    n_sb = samples_per_step
    steps_per_tile = (LANES // t) // n_sb
    with_sinks = sinks is not None
    kern = functools.partial(_window_attn_kernel, heads_per_step=heads_per_step, group=group, t=t, lb=lb,
                             n_sb=n_sb, with_sinks=with_sinks)
    o_spec = pl.BlockSpec((n_sb, t, cq_step), lambda n, hc, *_: (n, 0, hc))
    o_shape = jax.ShapeDtypeStruct((n_s, t, cq), F32)
    st_spec = pl.BlockSpec((n_sb, 2, rows, lb), lambda n, hc, *_: (n, 0, hc, 0))
    st_shape = jax.ShapeDtypeStruct(cache_t.shape, F32)
    in_specs = [
        pl.BlockSpec((n_sb, t, cq_step), lambda n, hc, *_: (n, 0, hc)),
        pl.BlockSpec((n_sb, 2, rows, lb), lambda n, hc, *_: (n, 0, hc, 0)),
        pl.BlockSpec((2, rows, LANES), lambda n, hc, *_: (0, hc, n // steps_per_tile)),
        pl.BlockSpec((heads_per_step * group, t, lb + LANES), lambda n, hc, *_: (hc, 0, 0)),
    ]
    grid = (n_s // n_sb, n_hc)
    if with_sinks:
        return pl.pallas_call(
            kern,
            grid_spec=pltpu.PrefetchScalarGridSpec(
                num_scalar_prefetch=1, grid=grid, in_specs=in_specs, out_specs=[o_spec, st_spec]),
            out_shape=[o_shape, st_shape], compiler_params=_cparams(2), name=name,
        )(sinks, q, cache_t, new_t, bias)
    return pl.pallas_call(
        kern, grid=grid, in_specs=in_specs, out_specs=[o_spec, o_spec, st_spec],
        out_shape=[o_shape, o_shape, st_shape], compiler_params=_cparams(2), name=name,
    )(q, cache_t, new_t, bias)


def _mix_out_kernel(*refs, n_groups, chunked):
    o_refs = refs[:n_groups]
    lse_refs = refs[n_groups:2 * n_groups] if n_groups > 1 else ()
    x_ref, wo_ref, g_ref, b_ref, out_ref = refs[len(o_refs) + len(lse_refs):]
    load = _load_chunked if chunked else (lambda r: r[...])
    if n_groups == 1:
        mixed = load(o_refs[0])
    else:
        lses = [load(r) for r in lse_refs]
        m = functools.reduce(jnp.maximum, lses)
        es = [jnp.exp(l - m) for l in lses]
        tot = functools.reduce(lambda a, b: a + b, es)
        mixed = functools.reduce(lambda a, b: a + b, [(e / tot) * load(r) for e, r in zip(es, o_refs)])
    y = _dot(mixed.astype(BF16), wo_ref[...])
    out_ref[...] = _layer_norm(ALPHA * x_ref[...] + y, g_ref[...], b_ref[...])


def mix_out(outs, lses, x, wo, gain, bias, tm=512):
    rows, d = x.shape
    tm = min(tm, rows)
    ng = len(outs)
    row_spec = pl.BlockSpec((tm, d), lambda i: (i, 0))
    chunked = outs[0].ndim == 4
    if chunked:
        per_seq = outs[0].shape[2] // tm
        mix_spec = pl.BlockSpec((1, N_CHUNKS, tm, LANES), lambda i: (i // per_seq, 0, i % per_seq, 0))
    else:
        mix_spec = row_spec
    return pl.pallas_call(
        functools.partial(_mix_out_kernel, n_groups=ng, chunked=chunked),
        grid=(rows // tm,),
        in_specs=[mix_spec] * (ng + len(lses)) + [row_spec, _resident(wo.shape), _resident((1, d)),
                                                   _resident((1, d))],
        out_specs=row_spec,
        out_shape=jax.ShapeDtypeStruct((rows, d), F32),
        compiler_params=_cparams(1),
        name="mix_out",
    )(*outs, *lses, x, wo, gain, bias)


WINDOW_BLOCK_BYTES = 4 * 1024 * 1024
MAX_SAMPLES_PER_STEP = 8


def _window_tiling(n_kv, lb):
    head_bytes = 2 * HEAD_DIM * lb * 4
    heads = max(1, min(n_kv, WINDOW_BLOCK_BYTES // head_bytes))
    samples = 1
    if heads == n_kv:
        samples = max(1, min(MAX_SAMPLES_PER_STEP, WINDOW_BLOCK_BYTES // (head_bytes * n_kv)))
    return heads, samples


def _to_pos_minor(cache):
    n, lb, _, h, dh = cache.shape
    return jnp.transpose(cache, (0, 2, 3, 4, 1)).reshape(n, 2, h * dh, lb)


def _from_pos_minor(state_t, n_heads):
    n, _, _, lb = state_t.shape
    return jnp.transpose(state_t.reshape(n, 2, n_heads, HEAD_DIM, lb), (0, 4, 1, 2, 3))[None]


def kernel(x_prompt, x_sample, cache_a_kv, cache_b1_kv, cache_b2_kv, cache_b3_kv, rel_bias_table, ln_gain, ln_bias, ffn_w_gate, ffn_w_up, ffn_w_down, attn_a_w_qkv, attn_a_w_o, attn_a_sinks, attn_b_w_qkv, attn_b_w_o):
    bsz, seq, dm = x_prompt.shape
    n_s, t, _ = x_sample.shape
    hd = N_HEADS * HEAD_DIM
    ck_a = N_KV_HEADS_A * HEAD_DIM
    xp = x_prompt.reshape(bsz * seq, dm)
    xs = x_sample.reshape(n_s * t, dm)
    table = rel_bias_table.astype(F32)
    li = 0

    def ln_params(i, j):
        return ln_gain[i, j].reshape(1, dm).astype(F32), ln_bias[i, j].reshape(1, dm).astype(F32)

    def half_ffn(x, i, f, j, **kw):
        g, b = ln_params(i, j)
        return ffn_ln(x, ffn_w_gate[i, f].astype(BF16), ffn_w_up[i, f].astype(BF16),
                      ffn_w_down[i, f].astype(BF16), g, b, **kw)

    def split_qkv(w, q_cols, kv_cols):
        wq = (w[:, q_cols[0]:q_cols[1]] * SCALE).astype(BF16)
        wkvt = w[:, kv_cols[0]:kv_cols[1]].T.astype(BF16)
        return wq, wkvt

    def sample_work(xs, cache, wq, wkvt, n_kv, dilation, units, name):
        xs3 = xs.reshape(1, n_s * t, dm)
        q, new_t = project(xs3, wq, wkvt, tl=n_s * t, l_off=0, l_len=n_s * t, kv_dtype=F32, q_dtype=F32,
                           name="proj_" + name)
        lb = cache.shape[1]
        heads, samples = _window_tiling(n_kv, lb)
        work = window_work(q.reshape(n_s, t, hd), _to_pos_minor(cache), new_t.reshape(2, n_kv * HEAD_DIM, n_s * t),
                           bias_lookup(table, _sample_bias_idx(dilation, units, lb, t)), heads)
        return work, samples

    def run_alone(work, samples, sinks, name):
        return window_attention(work["q"], work["cache_t"], work["new_t"], work["bias"], sinks,
                                heads_per_step=work["heads_per_step"], samples_per_step=samples, name="win_" + name)

    wq_a, wkvt_a = split_qkv(attn_a_w_qkv[li], (0, hd), (hd, hd + 2 * ck_a))
    wo_a = attn_a_w_o[li].astype(BF16)
    sinks = attn_a_sinks[li].astype(F32)
    w_qkv_b = attn_b_w_qkv[li]
    wo_b = attn_b_w_o[li].astype(BF16)
    w_b = []
    for gi in range(len(DILATED_GROUPS)):
        base = gi * 3 * hd
        w_b.append(split_qkv(w_qkv_b, (base, base + hd), (base + hd, base + 3 * hd)))
    caches_b = (cache_b1_kv[li], cache_b2_kv[li], cache_b3_kv[li])

    xs = half_ffn(xs, 0, 0, 0)
    work, samples = sample_work(xs, cache_a_kv[li], wq_a, wkvt_a, N_KV_HEADS_A, 1, WINDOW_A, "a_s")
    o, state_a_s = run_alone(work, samples, sinks, "a_s")
    state_a_s = _from_pos_minor(state_a_s, N_KV_HEADS_A)
    g1, b1 = ln_params(0, 1)
    xs = mix_out([o.reshape(n_s * t, hd)], [], xs, wo_a, g1, b1)
    xs = half_ffn(xs, 0, 1, 2)
    xs = half_ffn(xs, 1, 0, 0)
    outs_s, lses_s, states_s = [], [], []
    ride = None
    ffn_steps = (bsz * seq // min(512, bsz * seq)) * len(FFN_SPLITS)
    n_prompt_ffn = 2 * DEPTH
    for gi, (w, d) in enumerate(DILATED_GROUPS):
        work, samples = sample_work(xs, caches_b[gi], *w_b[gi], N_HEADS, d, w // d, f"b{gi}_s")
        if gi == len(DILATED_GROUPS) - 1 and samples == 1 and n_s * work["n_hc"] == n_prompt_ffn * ffn_steps:
            ride = work
            continue
        o, lse, st = run_alone(work, samples, None, f"b{gi}_s")
        outs_s.append(o.reshape(n_s * t, hd))
        lses_s.append(lse.reshape(n_s * t, hd))
        states_s.append(_from_pos_minor(st, N_HEADS))

    ride_parts = []

    def prompt_ffn(x, i, f, j, **kw):
        if ride is None:
            return half_ffn(x, i, f, j, **kw)
        k = len(ride_parts)
        per_call = n_s // n_prompt_ffn
        win = dict(ride, n0=k * per_call, n_count=per_call, prev_state=ride_parts[-1][2] if k else None)
        *res, o, lse, st = half_ffn(x, i, f, j, win=win, **kw)
        ride_parts.append((o, lse, st))
        return res if len(res) > 1 else res[0]

    xp = prompt_ffn(xp, 0, 0, 0)
    keep = min(WINDOW_A, seq)
    xp3 = xp.reshape(bsz, seq, dm)
    q, kvt = project(xp3, wq_a, wkvt_a, tl=512, l_off=0, l_len=seq, kv_dtype=BF16, name="proj_a")
    (st,) = project(xp3, None, wkvt_a, tl=keep, l_off=seq - keep, l_len=keep, kv_dtype=F32, name="state_a")
    state_a_p = _from_pos_minor(st.reshape(bsz, 2, ck_a, keep), N_KV_HEADS_A)
    o = band_attention(q, kvt, bias_lookup(table, _prompt_bias_idx(1, WINDOW_A)), sinks, name="attn_a")
    xp = mix_out([o], [], xp, wo_a, g1, b1)
    xp = prompt_ffn(xp, 0, 1, 2)

    xp, xp_chunked = prompt_ffn(xp, 1, 0, 0, chunked_seq=seq)
    g1, b1 = ln_params(1, 1)
    xp3 = xp.reshape(bsz, seq, dm)
    outs_p, lses_p, states_p = [], [], []
    for gi, (w, d) in enumerate(DILATED_GROUPS):
        wq, wkvt = w_b[gi]
        if d == 1:
            q, kvt = project(xp3, wq, wkvt, tl=512, l_off=0, l_len=seq, kv_dtype=BF16, name=f"proj_b{gi}")
        else:
            q, kvt = project_strided(xp_chunked, wq, wkvt, d=d, rows_per_step=512, name=f"proj_b{gi}")
        keep = min(w, seq)
        (st,) = project(xp3, None, wkvt, tl=min(keep, 512), l_off=seq - keep, l_len=keep, kv_dtype=F32,
                        name=f"state_b{gi}")
        states_p.append(_from_pos_minor(st.reshape(bsz, 2, hd, keep), N_HEADS))
        o, lse = band_attention(q, kvt, bias_lookup(table, _prompt_bias_idx(d, w // d)), None, name=f"attn_b{gi}")
        outs_p.append(o)
        lses_p.append(lse)
    xp = mix_out(outs_p, lses_p, xp, wo_b, g1, b1)
    xp = prompt_ffn(xp, 1, 1, 2)

    if ride is not None:
        outs_s.append(jnp.concatenate([p[0] for p in ride_parts], axis=0).reshape(n_s * t, hd))
        lses_s.append(jnp.concatenate([p[1] for p in ride_parts], axis=0).reshape(n_s * t, hd))
        states_s.append(_from_pos_minor(ride_parts[-1][2], N_HEADS))
    xs = mix_out(outs_s, lses_s, xs, wo_b, g1, b1)
    xs = half_ffn(xs, 1, 1, 2)

    return (xp.reshape(bsz, seq, dm), xs.reshape(n_s, t, dm),
            state_a_p, states_p[0], states_p[1], states_p[2],
            state_a_s, states_s[0], states_s[1], states_s[2])
```

```python
import functools
import math

import jax
import jax.numpy as jnp
from jax import lax
from jax.experimental import pallas as pl
from jax.experimental.pallas import tpu as pltpu

D_MODEL = 1024
HEAD_DIM = 64
N_HEADS = 16
N_KV_HEADS_A = 4
WINDOW_A = 128
DILATED_GROUPS = ((128, 1), (512, 4), (2048, 16))
BLOCK = 128
NUM_BUCKETS = 32
MAX_DISTANCE = 2048
DEPTH = 2
ALPHA = (2 * DEPTH) ** 0.25
LN_EPS = 1e-5
NEG_INF = -1e30
SCALE = HEAD_DIM ** -0.5

LANES = 128
N_CHUNKS = D_MODEL // LANES
V7X_VMEM_BYTES = 64 * 1024 * 1024
VMEM_LIMIT = 56 * 1024 * 1024
FFN_WIN_VMEM_LIMIT = 60 * 1024 * 1024

F32 = jnp.float32
BF16 = jnp.bfloat16


def _cparams(n_axes, vmem=VMEM_LIMIT):
    return pltpu.CompilerParams(dimension_semantics=("arbitrary",) * n_axes, vmem_limit_bytes=vmem)


def _resident(shape):
    nd = len(shape)
    return pl.BlockSpec(shape, lambda *_: (0,) * nd, pipeline_mode=pl.Buffered(1))


def _dot(a, b):
    return jnp.dot(a, b, preferred_element_type=F32)


def _dot_nt(a, b):
    return lax.dot_general(a, b, (((1,), (1,)), ((), ())), preferred_element_type=F32)


def _layer_norm(y, g, b):
    mu = jnp.mean(y, axis=-1, keepdims=True)
    yc = y - mu
    var = jnp.mean(yc * yc, axis=-1, keepdims=True)
    return yc * lax.rsqrt(var + LN_EPS) * g + b


FFN_SPLITS = (768, 768, 640, 640)


def _ffn_ln_kernel(*refs, splits, chunked, win):
    x_ref, wg_ref, wu_ref, wd_ref, g_ref, b_ref = refs[:6]
    pos = 6
    if win is not None:
        wq_ref, wcache_ref, wnew_ref, wbias_ref = refs[pos:pos + 4]
        pos += 4 + (1 if win["aliased"] else 0)
    o_ref = refs[pos]
    pos += 1
    if chunked:
        chunked_ref = refs[pos]
        pos += 1
    if win is not None:
        wo_ref, wlse_ref, wstate_ref = refs[pos:pos + 3]
        pos += 3
    xb_scr, acc_scr = refs[pos:pos + 2]
    i = pl.program_id(0)
    j = pl.program_id(1)
    last = len(splits) - 1
    off = 0
    for k, size in enumerate(splits):
        sl = slice(off, off + size)
        off += size

        @pl.when(j == k)
        def _(k=k, sl=sl):
            if k == 0:
                xb = x_ref[...].astype(BF16)
                xb_scr[...] = xb
            else:
                xb = xb_scr[...]
            gate = _dot(xb, wg_ref[:, sl])
            up = _dot(xb, wu_ref[:, sl])
            hid = gate * jax.nn.sigmoid(gate) * up
            part = _dot(hid.astype(BF16), wd_ref[sl, :])
            if k == 0:
                acc_scr[...] = part
            elif k < last:
                acc_scr[...] += part
            else:
                y = _layer_norm(ALPHA * x_ref[...] + 0.5 * (acc_scr[...] + part), g_ref[...], b_ref[...])
                o_ref[...] = y
                if chunked:
                    _store_chunked(chunked_ref, y)
            if win is not None:
                unit = i * len(splits) + j
                _window_unit(wq_ref, wcache_ref, wnew_ref, wbias_ref, wo_ref, wlse_ref, wstate_ref, None,
                             sb=0, n=win["n0"] + unit // win["n_hc"], hc=unit % win["n_hc"],
                             heads_per_step=win["heads_per_step"], group=win["group"], t=win["t"], lb=win["lb"])


def _store_chunked(ref, y, rows=None):
    for c in range(N_CHUNKS):
        if rows is None:
            ref[0, c] = y[:, c * LANES:(c + 1) * LANES]
        else:
            ref[0, c, rows, :] = y[:, c * LANES:(c + 1) * LANES]


def _load_chunked(ref, rows=None):
    if rows is None:
        return jnp.concatenate([ref[0, c] for c in range(N_CHUNKS)], axis=1)
    return jnp.concatenate([ref[0, c, rows, :] for c in range(N_CHUNKS)], axis=1)


def ffn_ln(x, wg, wu, wd, w_index, gain, bias, tm=512, chunked_seq=None, win=None):
    rows, d = x.shape
    d_ff = wg.shape[-1]
    tm = min(tm, rows)
    splits = FFN_SPLITS
    assert sum(splits) == d_ff
    nf = len(splits)

    def stacked(shape):
        return pl.BlockSpec((None, None) + shape, lambda i, j: w_index + (0, 0), pipeline_mode=pl.Buffered(1))

    in_specs = [
        pl.BlockSpec((tm, d), lambda i, j: (i, 0)),
        stacked((d, d_ff)), stacked((d, d_ff)), stacked((d_ff, d)),
        _resident((1, d)), _resident((1, d)),
    ]
    args = [x, wg, wu, wd, gain, bias]
    out_specs = [pl.BlockSpec((tm, d), lambda i, j: (i, 0))]
    out_shape = [jax.ShapeDtypeStruct((rows, d), F32)]
    if chunked_seq is not None:
        per_seq = chunked_seq // tm
        out_specs.append(pl.BlockSpec((1, N_CHUNKS, tm, LANES), lambda i, j: (i // per_seq, 0, i % per_seq, 0)))
        out_shape.append(jax.ShapeDtypeStruct((rows // chunked_seq, N_CHUNKS, chunked_seq, LANES), F32))
    win_static, aliases = _add_window(win, lambda i, j: i * nf + j, (rows // tm) * nf, in_specs, args, out_specs,
                                      out_shape)
    res = pl.pallas_call(
        functools.partial(_ffn_ln_kernel, splits=splits, chunked=chunked_seq is not None, win=win_static),
        grid=(rows // tm, nf),
        in_specs=in_specs,
        out_specs=out_specs,
        out_shape=out_shape,
        scratch_shapes=[pltpu.VMEM((tm, d), BF16), pltpu.VMEM((tm, d), F32)],
        input_output_aliases=aliases,
        compiler_params=_cparams(2, FFN_WIN_VMEM_LIMIT if win is not None else VMEM_LIMIT),
        name="ffn_ln_win" if win is not None else "ffn_ln",
    )(*args)
    return res if len(res) > 1 else res[0]


def _proj_kernel(*refs, with_q, d, lsub, res_per_step, win, steps_per_row):
    refs = list(refs)
    x_ref = refs.pop(0)
    wq_ref = refs.pop(0) if with_q else None
    wkvt_ref = refs.pop(0)
    if win is not None:
        win_in = [refs.pop(0) for _ in range(4)]
        if win["aliased"]:
            refs.pop(0)
    q_ref = refs.pop(0) if with_q else None
    kvt_ref = refs.pop(0)
    if win is not None:
        unit = pl.program_id(0) * steps_per_row + pl.program_id(1)
        _window_unit(*win_in, *refs, None, sb=0, n=win["n0"] + unit // win["n_hc"], hc=unit % win["n_hc"],
                     heads_per_step=win["heads_per_step"], group=win["group"], t=win["t"], lb=win["lb"])
    if d == 1:
        xb = x_ref[0].astype(BF16)
    else:
        r0 = pl.program_id(1) * res_per_step
        xb = jnp.concatenate(
            [_load_chunked(x_ref, pl.ds(r0 + rr, lsub, stride=d)) for rr in range(res_per_step)],
            axis=0).astype(BF16)
    if with_q:
        q = _dot(xb, wq_ref[...]).astype(q_ref.dtype)
        for rr in range(res_per_step):
            q_ref[0, rr] = q[rr * lsub:(rr + 1) * lsub]
    kvt = _dot_nt(wkvt_ref[...], xb).astype(kvt_ref.dtype)
    for rr in range(res_per_step):
        kvt_ref[0, rr] = kvt[:, rr * lsub:(rr + 1) * lsub]


def _add_window(win, unit_of, n_steps, in_specs, args, out_specs, out_shape):
    if win is None:
        return None, {}
    assert win["n_count"] * win["n_hc"] == n_steps
    w_in, w_out, w_shape = window_specs(win, unit_of)
    in_specs += w_in
    args += [win["q"], win["cache_t"], win["new_t"], win["bias"]]
    aliases = {}
    if win["prev_state"] is not None:
        in_specs.append(pl.BlockSpec(memory_space=pl.ANY))
        args.append(win["prev_state"])
        aliases = {len(args) - 1: len(out_specs) + 2}
    out_specs += w_out
    out_shape += w_shape
    static = dict(n0=win["n0"], n_hc=win["n_hc"], heads_per_step=win["heads_per_step"], group=win["group"],
                  t=win["t"], lb=win["lb"], aliased=win["prev_state"] is not None)
    return static, aliases


def project(x3, wq, wkvt, *, tl, l_off, l_len, kv_dtype, name, q_dtype=BF16, win=None):
    bsz, s, dm = x3.shape
    c2 = wkvt.shape[0]
    nl = l_len // tl
    off = l_off // tl
    with_q = wq is not None
    in_specs = [pl.BlockSpec((1, tl, dm), lambda b, j: (b, j + off, 0))]
    args = [x3]
    out_specs, out_shape = [], []
    if with_q:
        cq = wq.shape[1]
        in_specs.append(_resident(wq.shape))
        args.append(wq)
        out_specs.append(pl.BlockSpec((1, 1, tl, cq), lambda b, j: (b, 0, j, 0)))
        out_shape.append(jax.ShapeDtypeStruct((bsz, 1, l_len, cq), q_dtype))
    in_specs.append(_resident(wkvt.shape))
    args.append(wkvt)
    out_specs.append(pl.BlockSpec((1, 1, c2, tl), lambda b, j: (b, 0, 0, j)))
    out_shape.append(jax.ShapeDtypeStruct((bsz, 1, c2, l_len), kv_dtype))
    win_static, aliases = _add_window(win, lambda b, j: b * nl + j, bsz * nl, in_specs, args, out_specs, out_shape)
    return pl.pallas_call(
        functools.partial(_proj_kernel, with_q=with_q, d=1, lsub=tl, res_per_step=1, win=win_static,
                          steps_per_row=nl),
        grid=(bsz, nl),
        in_specs=in_specs, out_specs=out_specs, out_shape=out_shape,
        input_output_aliases=aliases,
        compiler_params=_cparams(2),
        name=name,
    )(*args)


def project_strided(xc, wq, wkvt, *, d, rows_per_step, name, win=None):
    bsz, _, s, _ = xc.shape
    lsub = s // d
    rps = max(1, rows_per_step // lsub)
    nj = d // rps
    c2, cq = wkvt.shape[0], wq.shape[1]
    in_specs = [pl.BlockSpec((1, N_CHUNKS, s, LANES), lambda b, j: (b, 0, 0, 0)),
                _resident(wq.shape), _resident(wkvt.shape)]
    args = [xc, wq, wkvt]
    out_specs = [pl.BlockSpec((1, rps, lsub, cq), lambda b, j: (b, j, 0, 0)),
                 pl.BlockSpec((1, rps, c2, lsub), lambda b, j: (b, j, 0, 0))]
    out_shape = [jax.ShapeDtypeStruct((bsz, d, lsub, cq), BF16),
                 jax.ShapeDtypeStruct((bsz, d, c2, lsub), BF16)]
    win_static, aliases = _add_window(win, lambda b, j: b * nj + j, bsz * nj, in_specs, args, out_specs, out_shape)
    return pl.pallas_call(
        functools.partial(_proj_kernel, with_q=True, d=d, lsub=lsub, res_per_step=rps, win=win_static,
                          steps_per_row=nj),
        grid=(bsz, nj),
        in_specs=in_specs, out_specs=out_specs, out_shape=out_shape,
        input_output_aliases=aliases,
        compiler_params=_cparams(2),
        name=name,
    )(*args)


def _bias_kernel(table_ref, idx_ref, o_ref):
    h = pl.program_id(0)
    idx = idx_ref[...]
    acc = jnp.full(idx.shape, NEG_INF, F32)
    for b in range(NUM_BUCKETS):
        acc = jnp.where(idx == b, table_ref[b * N_HEADS + h], acc)
    o_ref[0] = acc


def bias_lookup(table, idx):
    r, c = idx.shape
    return pl.pallas_call(
        _bias_kernel,
        grid_spec=pltpu.PrefetchScalarGridSpec(
            num_scalar_prefetch=1,
            grid=(N_HEADS,),
            in_specs=[pl.BlockSpec((r, c), lambda h, t: (0, 0))],
            out_specs=pl.BlockSpec((1, r, c), lambda h, t: (h, 0, 0)),
        ),
        out_shape=jax.ShapeDtypeStruct((N_HEADS, r, c), F32),
        name="bias_lookup",
    )(table.reshape(-1), idx)


def _t5_bucket(dist):
    max_exact = NUM_BUCKETS // 2
    d = jnp.maximum(dist.astype(F32), 1.0)
    large = max_exact + (jnp.log(d / max_exact) / math.log(MAX_DISTANCE / max_exact)
                         * (NUM_BUCKETS - max_exact)).astype(jnp.int32)
    large = jnp.minimum(large, NUM_BUCKETS - 1)
    return jnp.where(dist < max_exact, dist, large)


def _prompt_bias_idx(dilation, window_units):
    qi = jnp.arange(BLOCK)[:, None]
    ki = jnp.arange(2 * BLOCK)[None, :]
    rel = qi + BLOCK - ki
    band = (rel >= 0) & (rel <= window_units)
    return jnp.where(band, _t5_bucket(jnp.maximum(rel, 0) * dilation), -1).astype(jnp.int32)


def _sample_bias_idx(dilation, window_units, lb, t):
    tau = jnp.arange(t)[:, None]
    pos = jnp.arange(lb)[None, :]
    dist = lb + tau - pos
    ok = (dist % dilation == 0) & (dist // dilation <= window_units)
    idx_c = jnp.where(ok, _t5_bucket(dist), -1).astype(jnp.int32)
    nu = jnp.arange(LANES)[None, :] - (LANES - t)
    dist_n = tau - nu
    ok_n = (nu >= 0) & (dist_n >= 0) & (dist_n % dilation == 0) & (dist_n // dilation <= window_units)
    idx_n = jnp.where(ok_n, _t5_bucket(jnp.maximum(dist_n, 0)), -1).astype(jnp.int32)
    return jnp.concatenate([idx_c, idx_n], axis=1)


def _band_attn_kernel(*refs, n_kv_heads, n_blocks, d, with_sinks):
    if with_sinks:
        sink_ref, q_ref, kvt_ref, bias_ref, o_ref, s_scr, e_scr, f_scr, fac_scr = refs
    else:
        q_ref, kvt_ref, bias_ref, o_ref, lse_ref, s_scr, e_scr, f_scr, fac_scr, l_scr = refs
    r = pl.program_id(1)
    i = pl.program_id(2)
    ck = n_kv_heads * HEAD_DIM
    group = N_HEADS // n_kv_heads

    lane_lo = lax.broadcasted_iota(jnp.int32, (BLOCK, 2 * HEAD_DIM), 1) < HEAD_DIM

    def pair_tile(a, b):
        return jnp.where(lane_lo, a, b)

    def run(start, nk, bias_lo):
        for h in range(N_HEADS):
            kh = h // group
            qh = q_ref[0, 0, :, h * HEAD_DIM:(h + 1) * HEAD_DIM]
            kw = kvt_ref[0, 0, kh * HEAD_DIM:(kh + 1) * HEAD_DIM, pl.ds(start, nk)]
            s_scr[h, :, :nk] = _dot(qh, kw) + bias_ref[h, :, bias_lo:bias_lo + nk]
        for hp in range(N_HEADS // 2):
            lse_ab, fac_ab = [], []
            for h in (2 * hp, 2 * hp + 1):
                s = s_scr[h, :, :nk]
                m = jnp.max(s, axis=-1, keepdims=True)
                e = jnp.exp(s - m)
                den = jnp.sum(e, axis=-1, keepdims=True)
                e_scr[h, :, :nk] = e.astype(BF16)
                lse = m + jnp.log(den)
                fac = 1.0 / den
                if with_sinks:
                    fac = fac * jax.nn.sigmoid(lse - sink_ref[h])
                lse_ab.append(lse)
                fac_ab.append(fac)
            ps = slice(hp * 2 * HEAD_DIM, (hp + 1) * 2 * HEAD_DIM)
            fac_scr[:, ps] = pair_tile(*fac_ab)
            if not with_sinks:
                l_scr[:, ps] = pair_tile(*lse_ab)
        for hp in range(N_HEADS // 2):
            kh_a, kh_b = (2 * hp) // group, (2 * hp + 1) // group
            v_a = kvt_ref[0, 0, ck + kh_a * HEAD_DIM:ck + (kh_a + 1) * HEAD_DIM, pl.ds(start, nk)]
            if kh_a == kh_b:
                vw = jnp.concatenate([v_a, v_a], axis=0)
            else:
                vw = kvt_ref[0, 0, ck + kh_a * HEAD_DIM:ck + (kh_b + 1) * HEAD_DIM, pl.ds(start, nk)]
            ps = slice(hp * 2 * HEAD_DIM, (hp + 1) * 2 * HEAD_DIM)
            acc = jnp.where(lane_lo, _dot_nt(e_scr[2 * hp, :, :nk], vw), _dot_nt(e_scr[2 * hp + 1, :, :nk], vw))
            f_scr[:, ps] = acc * fac_scr[:, ps]

    if n_blocks == 1:
        run(0, BLOCK, BLOCK)
    else:
        @pl.when(i == 0)
        def _():
            run(0, BLOCK, BLOCK)

        @pl.when(i > 0)
        def _():
            run(pl.multiple_of((i - 1) * BLOCK, BLOCK), 2 * BLOCK, 0)

    rows = None if d == 1 else pl.ds(i * (BLOCK * d) + r, BLOCK, stride=d)
    _store_chunked(o_ref, f_scr[...], rows)
    if not with_sinks:
        _store_chunked(lse_ref, l_scr[...], rows)


def band_attention(q, kvt, bias, sinks, *, name):
    bsz, d, lsub, c = q.shape
    c2 = kvt.shape[2]
    nb = lsub // BLOCK
    with_sinks = sinks is not None
    kern = functools.partial(_band_attn_kernel, n_kv_heads=c2 // (2 * HEAD_DIM), n_blocks=nb, d=d,
                             with_sinks=with_sinks)
    if d == 1:
        o_spec = pl.BlockSpec((1, N_CHUNKS, BLOCK, LANES), lambda b, r, i, *_: (b, 0, i, 0))
    else:
        o_spec = pl.BlockSpec((1, N_CHUNKS, lsub * d, LANES), lambda b, r, i, *_: (b, 0, 0, 0))
    o_shape = jax.ShapeDtypeStruct((bsz, N_CHUNKS, lsub * d, LANES), F32)
    in_specs = [
        pl.BlockSpec((1, 1, BLOCK, c), lambda b, r, i, *_: (b, r, i, 0)),
        pl.BlockSpec((1, 1, c2, lsub), lambda b, r, i, *_: (b, r, 0, 0)),
        pl.BlockSpec(bias.shape, lambda b, r, i, *_: (0, 0, 0), pipeline_mode=pl.Buffered(1)),
    ]
    scratch = [pltpu.VMEM((N_HEADS, BLOCK, 2 * BLOCK), F32), pltpu.VMEM((N_HEADS, BLOCK, 2 * BLOCK), BF16),
               pltpu.VMEM((BLOCK, c), F32), pltpu.VMEM((BLOCK, c), F32)]
    grid = (bsz, d, nb)
    if with_sinks:
        return pl.pallas_call(
            kern,
            grid_spec=pltpu.PrefetchScalarGridSpec(
                num_scalar_prefetch=1, grid=grid, in_specs=in_specs, out_specs=o_spec, scratch_shapes=scratch),
            out_shape=o_shape, compiler_params=_cparams(3), name=name,
        )(sinks, q, kvt, bias)
    return pl.pallas_call(
        kern, grid=grid, in_specs=in_specs, out_specs=[o_spec, o_spec], out_shape=[o_shape, o_shape],
        scratch_shapes=scratch + [pltpu.VMEM((BLOCK, c), F32)],
        compiler_params=_cparams(3), name=name,
    )(q, kvt, bias)


def _window_unit(q_ref, cache_ref, new_ref, bias_ref, o_ref, lse_ref, state_ref, sink_ref, *,
                 sb, n, hc, heads_per_step, group, t, lb):
    rows = heads_per_step * HEAD_DIM
    per_tile = LANES // t
    lane = lax.broadcasted_iota(jnp.int32, (rows, LANES), 1)
    keep = lane < LANES - t
    n_tiles = lb // LANES
    shift = (LANES - t) - (n % per_tile) * t
    new_k = pltpu.roll(new_ref[0], shift, 1)
    new_v = pltpu.roll(new_ref[1], shift, 1)
    new_kb, new_vb = new_k.astype(BF16), new_v.astype(BF16)
    scores = []
    for kh in range(heads_per_step):
        rr = slice(kh * HEAD_DIM, (kh + 1) * HEAD_DIM)
        kt = jnp.concatenate([cache_ref[sb, 0, rr, :].astype(BF16), new_kb[rr]], axis=1)
        for g in range(group):
            hl = kh * group + g
            qh = q_ref[sb, :, hl * HEAD_DIM:(hl + 1) * HEAD_DIM].astype(BF16)
            scores.append(_dot(qh, kt) + bias_ref[hl])
    probs, lses = [], []
    for s in scores:
        m = jnp.max(s, axis=-1, keepdims=True)
        e = jnp.exp(s - m)
        den = jnp.sum(e, axis=-1, keepdims=True)
        probs.append((e / den).astype(BF16))
        lses.append(m + jnp.log(den))
    for kh in range(heads_per_step):
        rr = slice(kh * HEAD_DIM, (kh + 1) * HEAD_DIM)
        vt = jnp.concatenate([cache_ref[sb, 1, rr, :].astype(BF16), new_vb[rr]], axis=1)
        for g in range(group):
            hl = kh * group + g
            hs = slice(hl * HEAD_DIM, (hl + 1) * HEAD_DIM)
            out = _dot_nt(probs[hl], vt)
            if sink_ref is not None:
                h_abs = hc * (heads_per_step * group) + hl
                out = out * jax.nn.sigmoid(lses[hl] - sink_ref[h_abs])
            else:
                lse_ref[sb, :, hs] = jnp.broadcast_to(lses[hl], (t, HEAD_DIM))
            o_ref[sb, :, hs] = out
    for kv, new in ((0, new_k), (1, new_v)):
        rolled = pltpu.roll(cache_ref[sb, kv, :, 0:LANES], LANES - t, 1)
        for j in range(n_tiles):
            if j + 1 < n_tiles:
                nxt = pltpu.roll(cache_ref[sb, kv, :, (j + 1) * LANES:(j + 2) * LANES], LANES - t, 1)
            else:
                nxt = new
            state_ref[sb, kv, :, j * LANES:(j + 1) * LANES] = jnp.where(keep, rolled, nxt)
            rolled = nxt


def _window_attn_kernel(*refs, heads_per_step, group, t, lb, n_sb, with_sinks):
    if with_sinks:
        sink_ref, q_ref, cache_ref, new_ref, bias_ref, o_ref, state_ref = refs
        lse_ref = None
    else:
        q_ref, cache_ref, new_ref, bias_ref, o_ref, lse_ref, state_ref = refs
        sink_ref = None
    for sb in range(n_sb):
        _window_unit(q_ref, cache_ref, new_ref, bias_ref, o_ref, lse_ref, state_ref, sink_ref,
                     sb=sb, n=pl.program_id(0) * n_sb + sb, hc=pl.program_id(1),
                     heads_per_step=heads_per_step, group=group, t=t, lb=lb)


def window_work(q, cache_t, new_t, bias, heads_per_step):
    n_s, t, cq = q.shape
    _, _, ck, lb = cache_t.shape
    n_kv = ck // HEAD_DIM
    return dict(q=q, cache_t=cache_t, new_t=new_t, bias=bias, t=t, lb=lb, n_total=n_s,
                heads_per_step=heads_per_step, group=(cq // HEAD_DIM) // n_kv, n_hc=n_kv // heads_per_step,
                n0=0, n_count=n_s, prev_state=None)


def window_specs(win, unit_of):
    t, lb, n_hc, n0 = win["t"], win["lb"], win["n_hc"], win["n0"]
    rows = win["heads_per_step"] * HEAD_DIM
    cq_step = rows * win["group"]
    per_tile = LANES // t

    def samp(*idx):
        return n0 + unit_of(*idx) // n_hc

    def chunk(*idx):
        return unit_of(*idx) % n_hc

    in_specs = [
        pl.BlockSpec((1, t, cq_step), lambda *idx: (samp(*idx), 0, chunk(*idx))),
        pl.BlockSpec((1, 2, rows, lb), lambda *idx: (samp(*idx), 0, chunk(*idx), 0)),
        pl.BlockSpec((2, rows, LANES), lambda *idx: (0, chunk(*idx), samp(*idx) // per_tile)),
        pl.BlockSpec((win["heads_per_step"] * win["group"], t, lb + LANES), lambda *idx: (chunk(*idx), 0, 0)),
    ]
    part_spec = pl.BlockSpec((1, t, cq_step), lambda *idx: (samp(*idx) - n0, 0, chunk(*idx)))
    part_shape = jax.ShapeDtypeStruct((win["n_count"], t, cq_step * n_hc), F32)
    out_specs = [part_spec, part_spec, pl.BlockSpec((1, 2, rows, lb), lambda *idx: (samp(*idx), 0, chunk(*idx), 0))]
    out_shape = [part_shape, part_shape, jax.ShapeDtypeStruct(win["cache_t"].shape, F32)]
    return in_specs, out_specs, out_shape


def window_attention(q, cache_t, new_t, bias, sinks, *, heads_per_step, samples_per_step, name):
    n_s, t, cq = q.shape
    _, _, ck, lb = cache_t.shape
    n_kv = ck // HEAD_DIM
    group = (cq // HEAD_DIM) // n_kv
    n_hc = n_kv // heads_per_step
    rows = heads_per_step * HEAD_DIM
    cq_step = rows * group
    n_sb = samples_per_step
    steps_per_tile = (LANES // t) // n_sb
    with_sinks = sinks is not None
    kern = functools.partial(_window_attn_kernel, heads_per_step=heads_per_step, group=group, t=t, lb=lb,
                             n_sb=n_sb, with_sinks=with_sinks)
    o_spec = pl.BlockSpec((n_sb, t, cq_step), lambda n, hc, *_: (n, 0, hc))
    o_shape = jax.ShapeDtypeStruct((n_s, t, cq), F32)
    st_spec = pl.BlockSpec((n_sb, 2, rows, lb), lambda n, hc, *_: (n, 0, hc, 0))
    st_shape = jax.ShapeDtypeStruct(cache_t.shape, F32)
    in_specs = [
        pl.BlockSpec((n_sb, t, cq_step), lambda n, hc, *_: (n, 0, hc)),
        pl.BlockSpec((n_sb, 2, rows, lb), lambda n, hc, *_: (n, 0, hc, 0)),
        pl.BlockSpec((2, rows, LANES), lambda n, hc, *_: (0, hc, n // steps_per_tile)),
        pl.BlockSpec((heads_per_step * group, t, lb + LANES), lambda n, hc, *_: (hc, 0, 0)),
    ]
    grid = (n_s // n_sb, n_hc)
    if with_sinks:
        return pl.pallas_call(
            kern,
            grid_spec=pltpu.PrefetchScalarGridSpec(
                num_scalar_prefetch=1, grid=grid, in_specs=in_specs, out_specs=[o_spec, st_spec]),
            out_shape=[o_shape, st_shape], compiler_params=_cparams(2), name=name,
        )(sinks, q, cache_t, new_t, bias)
    return pl.pallas_call(
        kern, grid=grid, in_specs=in_specs, out_specs=[o_spec, o_spec, st_spec],
        out_shape=[o_shape, o_shape, st_shape], compiler_params=_cparams(2), name=name,
    )(q, cache_t, new_t, bias)


def _mix_out_kernel(*refs, n_groups, chunked):
    o_refs = refs[:n_groups]
    lse_refs = refs[n_groups:2 * n_groups] if n_groups > 1 else ()
    x_ref, wo_ref, g_ref, b_ref, out_ref = refs[len(o_refs) + len(lse_refs):]
    load = _load_chunked if chunked else (lambda r: r[...])
    if n_groups == 1:
        mixed = load(o_refs[0])
    else:
        lses = [load(r) for r in lse_refs]
        m = functools.reduce(jnp.maximum, lses)
        es = [jnp.exp(l - m) for l in lses]
        tot = functools.reduce(lambda a, b: a + b, es)
        mixed = functools.reduce(lambda a, b: a + b, [(e / tot) * load(r) for e, r in zip(es, o_refs)])
    y = _dot(mixed.astype(BF16), wo_ref[...])
    out_ref[...] = _layer_norm(ALPHA * x_ref[...] + y, g_ref[...], b_ref[...])


def mix_out(outs, lses, x, wo, gain, bias, tm=512):
    rows, d = x.shape
    tm = min(tm, rows)
    ng = len(outs)
    row_spec = pl.BlockSpec((tm, d), lambda i: (i, 0))
    chunked = outs[0].ndim == 4
    if chunked:
        per_seq = outs[0].shape[2] // tm
        mix_spec = pl.BlockSpec((1, N_CHUNKS, tm, LANES), lambda i: (i // per_seq, 0, i % per_seq, 0))
    else:
        mix_spec = row_spec
    return pl.pallas_call(
        functools.partial(_mix_out_kernel, n_groups=ng, chunked=chunked),
        grid=(rows // tm,),
        in_specs=[mix_spec] * (ng + len(lses)) + [row_spec, _resident(wo.shape), _resident((1, d)),
                                                   _resident((1, d))],
        out_specs=row_spec,
        out_shape=jax.ShapeDtypeStruct((rows, d), F32),
        compiler_params=_cparams(1),
        name="mix_out",
    )(*outs, *lses, x, wo, gain, bias)


WINDOW_BLOCK_BYTES = 4 * 1024 * 1024
MAX_SAMPLES_PER_STEP = 8


def _window_tiling(n_kv, lb):
    head_bytes = 2 * HEAD_DIM * lb * 4
    heads = max(1, min(n_kv, WINDOW_BLOCK_BYTES // head_bytes))
    samples = 1
    if heads == n_kv:
        samples = max(1, min(MAX_SAMPLES_PER_STEP, WINDOW_BLOCK_BYTES // (head_bytes * n_kv)))
    return heads, samples


def _to_pos_minor(cache):
    n, lb, _, h, dh = cache.shape
    return jnp.transpose(cache, (0, 2, 3, 4, 1)).reshape(n, 2, h * dh, lb)


def _from_pos_minor(state_t, n_heads):
    n, _, _, lb = state_t.shape
    return jnp.transpose(state_t.reshape(n, 2, n_heads, HEAD_DIM, lb), (0, 4, 1, 2, 3))[None]


def kernel(x_prompt, x_sample, cache_a_kv, cache_b1_kv, cache_b2_kv, cache_b3_kv, rel_bias_table, ln_gain, ln_bias, ffn_w_gate, ffn_w_up, ffn_w_down, attn_a_w_qkv, attn_a_w_o, attn_a_sinks, attn_b_w_qkv, attn_b_w_o):
    bsz, seq, dm = x_prompt.shape
    n_s, t, _ = x_sample.shape
    hd = N_HEADS * HEAD_DIM
    ck_a = N_KV_HEADS_A * HEAD_DIM
    xp = x_prompt.reshape(bsz * seq, dm)
    xs = x_sample.reshape(n_s * t, dm)
    table = rel_bias_table.astype(F32)
    li = 0

    def ln_params(i, j):
        return ln_gain[i, j].reshape(1, dm).astype(F32), ln_bias[i, j].reshape(1, dm).astype(F32)

    wg_all, wu_all, wd_all = ffn_w_gate.astype(BF16), ffn_w_up.astype(BF16), ffn_w_down.astype(BF16)

    def half_ffn(x, i, f, j, **kw):
        g, b = ln_params(i, j)
        return ffn_ln(x, wg_all, wu_all, wd_all, (i, f), g, b, **kw)

    def split_qkv(w, q_cols, kv_cols):
        wq = (w[:, q_cols[0]:q_cols[1]] * SCALE).astype(BF16)
        wkvt = w[:, kv_cols[0]:kv_cols[1]].T.astype(BF16)
        return wq, wkvt

    def sample_work(xs, cache, wq, wkvt, n_kv, dilation, units, name):
        xs3 = xs.reshape(1, n_s * t, dm)
        q, new_t = project(xs3, wq, wkvt, tl=n_s * t, l_off=0, l_len=n_s * t, kv_dtype=F32, q_dtype=F32,
                           name="proj_" + name)
        lb = cache.shape[1]
        heads, samples = _window_tiling(n_kv, lb)
        work = window_work(q.reshape(n_s, t, hd), _to_pos_minor(cache), new_t.reshape(2, n_kv * HEAD_DIM, n_s * t),
                           bias_lookup(table, _sample_bias_idx(dilation, units, lb, t)), heads)
        return work, samples

    def run_alone(work, samples, sinks, name):
        return window_attention(work["q"], work["cache_t"], work["new_t"], work["bias"], sinks,
                                heads_per_step=work["heads_per_step"], samples_per_step=samples, name="win_" + name)

    wq_a, wkvt_a = split_qkv(attn_a_w_qkv[li], (0, hd), (hd, hd + 2 * ck_a))
    wo_a = attn_a_w_o[li].astype(BF16)
    sinks = attn_a_sinks[li].astype(F32)
    w_qkv_b = attn_b_w_qkv[li]
    wo_b = attn_b_w_o[li].astype(BF16)
    w_b = []
    for gi in range(len(DILATED_GROUPS)):
        base = gi * 3 * hd
        w_b.append(split_qkv(w_qkv_b, (base, base + hd), (base + hd, base + 3 * hd)))
    caches_b = (cache_b1_kv[li], cache_b2_kv[li], cache_b3_kv[li])

    xs = half_ffn(xs, 0, 0, 0)
    work, samples = sample_work(xs, cache_a_kv[li], wq_a, wkvt_a, N_KV_HEADS_A, 1, WINDOW_A, "a_s")
    o, state_a_s = run_alone(work, samples, sinks, "a_s")
    state_a_s = _from_pos_minor(state_a_s, N_KV_HEADS_A)
    g1, b1 = ln_params(0, 1)
    xs = mix_out([o.reshape(n_s * t, hd)], [], xs, wo_a, g1, b1)
    xs = half_ffn(xs, 0, 1, 2)
    xs = half_ffn(xs, 1, 0, 0)
    n_prompt_ffn = 2 * DEPTH
    ffn_steps = (bsz * seq // min(512, bsz * seq)) * len(FFN_SPLITS)
    proj_steps = bsz * (seq // 512)
    carriers = {1: ("proj", len(DILATED_GROUPS) + 1, proj_steps), 2: ("ffn", n_prompt_ffn, ffn_steps)}
    riders = {}
    sample_res = {}
    for gi, (w, d) in enumerate(DILATED_GROUPS):
        work, samples = sample_work(xs, caches_b[gi], *w_b[gi], N_HEADS, d, w // d, f"b{gi}_s")
        kind, n_calls, steps = carriers.get(gi, (None, 1, 0))
        if kind is not None and samples == 1 and n_s * work["n_hc"] == n_calls * steps:
            riders[kind] = dict(gi=gi, work=work, n_calls=n_calls, parts=[])
        else:
            sample_res[gi] = run_alone(work, samples, None, f"b{gi}_s")

    def carry(kind, fn, *args, **kw):
        rider = riders.get(kind)
        if rider is None:
            return fn(*args, **kw)
        k = len(rider["parts"])
        per_call = n_s // rider["n_calls"]
        win = dict(rider["work"], n0=k * per_call, n_count=per_call,
                   prev_state=rider["parts"][-1][2] if k else None)
        *res, o, lse, st = fn(*args, win=win, **kw)
        rider["parts"].append((o, lse, st))
        return res if len(res) > 1 else res[0]

    xp = carry("ffn", half_ffn, xp, 0, 0, 0)
    keep = min(WINDOW_A, seq)
    xp3 = xp.reshape(bsz, seq, dm)
    q, kvt = project(xp3, wq_a, wkvt_a, tl=512, l_off=0, l_len=seq, kv_dtype=BF16, name="proj_a")
    (st,) = project(xp3, None, wkvt_a, tl=keep, l_off=seq - keep, l_len=keep, kv_dtype=F32, name="state_a")
    state_a_p = _from_pos_minor(st.reshape(bsz, 2, ck_a, keep), N_KV_HEADS_A)
    o = band_attention(q, kvt, bias_lookup(table, _prompt_bias_idx(1, WINDOW_A)), sinks, name="attn_a")
    xp = mix_out([o], [], xp, wo_a, g1, b1)
    xp = carry("ffn", half_ffn, xp, 0, 1, 2)

    xp, xp_chunked = carry("ffn", half_ffn, xp, 1, 0, 0, chunked_seq=seq)
    g1, b1 = ln_params(1, 1)
    xp3 = xp.reshape(bsz, seq, dm)
    outs_p, lses_p, states_p = [], [], []
    for gi, (w, d) in enumerate(DILATED_GROUPS):
        wq, wkvt = w_b[gi]
        if d == 1:
            q, kvt = carry("proj", project, xp3, wq, wkvt, tl=512, l_off=0, l_len=seq, kv_dtype=BF16,
                           name=f"proj_b{gi}")
        else:
            q, kvt = carry("proj", project_strided, xp_chunked, wq, wkvt, d=d, rows_per_step=512,
                           name=f"proj_b{gi}")
        keep = min(w, seq)
        state_kw = dict(tl=min(keep, 512), l_off=seq - keep, l_len=keep, kv_dtype=F32, name=f"state_b{gi}")
        if keep == seq:
            st = carry("proj", project, xp3, None, wkvt, **state_kw)
            st = st[0] if isinstance(st, (list, tuple)) else st
        else:
            (st,) = project(xp3, None, wkvt, **state_kw)
        states_p.append(_from_pos_minor(st.reshape(bsz, 2, hd, keep), N_HEADS))
        o, lse = band_attention(q, kvt, bias_lookup(table, _prompt_bias_idx(d, w // d)), None, name=f"attn_b{gi}")
        outs_p.append(o)
        lses_p.append(lse)
    xp = mix_out(outs_p, lses_p, xp, wo_b, g1, b1)
    xp = carry("ffn", half_ffn, xp, 1, 1, 2)

    for rider in riders.values():
        parts = rider["parts"]
        assert len(parts) == rider["n_calls"]
        sample_res[rider["gi"]] = (jnp.concatenate([p[0] for p in parts], axis=0),
                                   jnp.concatenate([p[1] for p in parts], axis=0), parts[-1][2])
    outs_s = [sample_res[gi][0].reshape(n_s * t, hd) for gi in range(len(DILATED_GROUPS))]
    lses_s = [sample_res[gi][1].reshape(n_s * t, hd) for gi in range(len(DILATED_GROUPS))]
    states_s = [_from_pos_minor(sample_res[gi][2], N_HEADS) for gi in range(len(DILATED_GROUPS))]
    xs = mix_out(outs_s, lses_s, xs, wo_b, g1, b1)
    xs = half_ffn(xs, 1, 1, 2)

    return (xp.reshape(bsz, seq, dm), xs.reshape(n_s, t, dm),
            state_a_p, states_p[0], states_p[1], states_p[2],
            state_a_s, states_s[0], states_s[1], states_s[2])
```

```python
import functools
import math

import jax
import jax.numpy as jnp
from jax import lax
from jax.experimental import pallas as pl
from jax.experimental.pallas import tpu as pltpu

D_MODEL = 1024
HEAD_DIM = 64
N_HEADS = 16
N_KV_HEADS_A = 4
WINDOW_A = 128
DILATED_GROUPS = ((128, 1), (512, 4), (2048, 16))
BLOCK = 128
NUM_BUCKETS = 32
MAX_DISTANCE = 2048
DEPTH = 2
ALPHA = (2 * DEPTH) ** 0.25
LN_EPS = 1e-5
NEG_INF = -1e30
SCALE = HEAD_DIM ** -0.5

LANES = 128
N_CHUNKS = D_MODEL // LANES
V7X_VMEM_BYTES = 64 * 1024 * 1024
VMEM_LIMIT = 56 * 1024 * 1024
FFN_WIN_VMEM_LIMIT = 60 * 1024 * 1024

F32 = jnp.float32
BF16 = jnp.bfloat16


def _cparams(n_axes, vmem=VMEM_LIMIT):
    return pltpu.CompilerParams(dimension_semantics=("arbitrary",) * n_axes, vmem_limit_bytes=vmem)


def _resident(shape):
    nd = len(shape)
    return pl.BlockSpec(shape, lambda *_: (0,) * nd, pipeline_mode=pl.Buffered(1))


def _dot(a, b):
    return jnp.dot(a, b, preferred_element_type=F32)


def _dot_nt(a, b):
    return lax.dot_general(a, b, (((1,), (1,)), ((), ())), preferred_element_type=F32)


def _layer_norm(y, g, b):
    mu = jnp.mean(y, axis=-1, keepdims=True)
    yc = y - mu
    var = jnp.mean(yc * yc, axis=-1, keepdims=True)
    return yc * lax.rsqrt(var + LN_EPS) * g + b


FFN_SPLITS = (768, 768, 640, 640)


def _ffn_ln_kernel(*refs, splits, chunked, win, pre):
    x_ref, wg_ref, wu_ref, wd_ref, g_ref, b_ref = refs[:6]
    pos = 6
    if pre:
        mix_ref, wproj_ref, g1_ref, b1_ref = refs[pos:pos + 4]
        pos += 4
    if win is not None:
        wq_ref, wcache_ref, wnew_ref, wbias_ref = refs[pos:pos + 4]
        pos += 4 + (1 if win["aliased"] else 0)
    o_ref = refs[pos]
    pos += 1
    if chunked:
        chunked_ref = refs[pos]
        pos += 1
    if win is not None:
        wo_ref, wlse_ref, wstate_ref = refs[pos:pos + 3]
        pos += 3
    xb_scr, acc_scr = refs[pos:pos + 2]
    xin_scr = refs[pos + 2] if pre else None
    i = pl.program_id(0)
    j = pl.program_id(1)
    last = len(splits) - 1
    off = 0
    for k, size in enumerate(splits):
        sl = slice(off, off + size)
        off += size

        @pl.when(j == k)
        def _(k=k, sl=sl):
            if k == 0 and pre:
                x_in = _layer_norm(ALPHA * x_ref[...] + _dot(_load_chunked(mix_ref).astype(BF16), wproj_ref[...]),
                                   g1_ref[...], b1_ref[...])
                xin_scr[...] = x_in
                xb = x_in.astype(BF16)
                xb_scr[...] = xb
            elif k == 0:
                xb = x_ref[...].astype(BF16)
                xb_scr[...] = xb
            else:
                xb = xb_scr[...]
            gate = _dot(xb, wg_ref[:, sl])
            up = _dot(xb, wu_ref[:, sl])
            hid = gate * jax.nn.sigmoid(gate) * up
            part = _dot(hid.astype(BF16), wd_ref[sl, :])
            if k == 0:
                acc_scr[...] = part
            elif k < last:
                acc_scr[...] += part
            else:
                x_in = xin_scr[...] if pre else x_ref[...]
                y = _layer_norm(ALPHA * x_in + 0.5 * (acc_scr[...] + part), g_ref[...], b_ref[...])
                o_ref[...] = y
                if chunked:
                    _store_chunked(chunked_ref, y)
            if win is not None:
                unit = i * len(splits) + j
                _window_unit(wq_ref, wcache_ref, wnew_ref, wbias_ref, wo_ref, wlse_ref, wstate_ref, None,
                             sb=0, n=win["n0"] + unit // win["n_hc"], hc=unit % win["n_hc"],
                             heads_per_step=win["heads_per_step"], group=win["group"], t=win["t"], lb=win["lb"])


def _store_chunked(ref, y, rows=None):
    for c in range(N_CHUNKS):
        if rows is None:
            ref[0, c] = y[:, c * LANES:(c + 1) * LANES]
        else:
            ref[0, c, rows, :] = y[:, c * LANES:(c + 1) * LANES]


def _load_chunked(ref, rows=None):
    if rows is None:
        return jnp.concatenate([ref[0, c] for c in range(N_CHUNKS)], axis=1)
    return jnp.concatenate([ref[0, c, rows, :] for c in range(N_CHUNKS)], axis=1)


def ffn_ln(x, wg, wu, wd, w_index, gain, bias, tm=512, chunked_seq=None, win=None, pre=None):
    rows, d = x.shape
    d_ff = wg.shape[-1]
    tm = min(tm, rows)
    splits = FFN_SPLITS
    assert sum(splits) == d_ff
    nf = len(splits)

    def stacked(shape):
        return pl.BlockSpec((None, None) + shape, lambda i, j: w_index + (0, 0), pipeline_mode=pl.Buffered(1))

    in_specs = [
        pl.BlockSpec((tm, d), lambda i, j: (i, 0)),
        stacked((d, d_ff)), stacked((d, d_ff)), stacked((d_ff, d)),
        _resident((1, d)), _resident((1, d)),
    ]
    args = [x, wg, wu, wd, gain, bias]
    scratch = [pltpu.VMEM((tm, d), BF16), pltpu.VMEM((tm, d), F32)]
    if pre is not None:
        mix, wo, gain1, bias1 = pre
        per_mix = mix.shape[2] // tm
        in_specs += [pl.BlockSpec((1, N_CHUNKS, tm, LANES), lambda i, j: (i // per_mix, 0, i % per_mix, 0)),
                     _resident(wo.shape), _resident((1, d)), _resident((1, d))]
        args += [mix, wo, gain1, bias1]
        scratch.append(pltpu.VMEM((tm, d), F32))
    out_specs = [pl.BlockSpec((tm, d), lambda i, j: (i, 0))]
    out_shape = [jax.ShapeDtypeStruct((rows, d), F32)]
    if chunked_seq is not None:
        per_seq = chunked_seq // tm
        out_specs.append(pl.BlockSpec((1, N_CHUNKS, tm, LANES), lambda i, j: (i // per_seq, 0, i % per_seq, 0)))
        out_shape.append(jax.ShapeDtypeStruct((rows // chunked_seq, N_CHUNKS, chunked_seq, LANES), F32))
    win_static, aliases = _add_window(win, lambda i, j: i * nf + j, (rows // tm) * nf, in_specs, args, out_specs,
                                      out_shape)
    res = pl.pallas_call(
        functools.partial(_ffn_ln_kernel, splits=splits, chunked=chunked_seq is not None, win=win_static,
                          pre=pre is not None),
        grid=(rows // tm, nf),
        in_specs=in_specs,
        out_specs=out_specs,
        out_shape=out_shape,
        scratch_shapes=scratch,
        input_output_aliases=aliases,
        compiler_params=_cparams(2, FFN_WIN_VMEM_LIMIT if win is not None else VMEM_LIMIT),
        name="ffn_ln_win" if win is not None else "ffn_ln",
    )(*args)
    return res if len(res) > 1 else res[0]


def _proj_kernel(*refs, with_q, d, lsub, res_per_step, win, steps_per_row):
    refs = list(refs)
    x_ref = refs.pop(0)
    wq_ref = refs.pop(0) if with_q else None
    wkvt_ref = refs.pop(0)
    if win is not None:
        win_in = [refs.pop(0) for _ in range(4)]
        if win["aliased"]:
            refs.pop(0)
    q_ref = refs.pop(0) if with_q else None
    kvt_ref = refs.pop(0)
    if win is not None:
        unit = pl.program_id(0) * steps_per_row + pl.program_id(1)
        _window_unit(*win_in, *refs, None, sb=0, n=win["n0"] + unit // win["n_hc"], hc=unit % win["n_hc"],
                     heads_per_step=win["heads_per_step"], group=win["group"], t=win["t"], lb=win["lb"])
    if d == 1:
        xb = x_ref[0].astype(BF16)
    else:
        r0 = pl.program_id(1) * res_per_step
        xb = jnp.concatenate(
            [_load_chunked(x_ref, pl.ds(r0 + rr, lsub, stride=d)) for rr in range(res_per_step)],
            axis=0).astype(BF16)
    if with_q:
        q = _dot(xb, wq_ref[...]).astype(q_ref.dtype)
        for rr in range(res_per_step):
            q_ref[0, rr] = q[rr * lsub:(rr + 1) * lsub]
    kvt = _dot_nt(wkvt_ref[...], xb).astype(kvt_ref.dtype)
    for rr in range(res_per_step):
        kvt_ref[0, rr] = kvt[:, rr * lsub:(rr + 1) * lsub]


def _add_window(win, unit_of, n_steps, in_specs, args, out_specs, out_shape):
    if win is None:
        return None, {}
    assert win["n_count"] * win["n_hc"] == n_steps
    w_in, w_out, w_shape = window_specs(win, unit_of)
    in_specs += w_in
    args += [win["q"], win["cache_t"], win["new_t"], win["bias"]]
    aliases = {}
    if win["prev_state"] is not None:
        in_specs.append(pl.BlockSpec(memory_space=pl.ANY))
        args.append(win["prev_state"])
        aliases = {len(args) - 1: len(out_specs) + 2}
    out_specs += w_out
    out_shape += w_shape
    static = dict(n0=win["n0"], n_hc=win["n_hc"], heads_per_step=win["heads_per_step"], group=win["group"],
                  t=win["t"], lb=win["lb"], aliased=win["prev_state"] is not None)
    return static, aliases


def project(x3, wq, wkvt, *, tl, l_off, l_len, kv_dtype, name, q_dtype=BF16, win=None):
    bsz, s, dm = x3.shape
    c2 = wkvt.shape[0]
    nl = l_len // tl
    off = l_off // tl
    with_q = wq is not None
    in_specs = [pl.BlockSpec((1, tl, dm), lambda b, j: (b, j + off, 0))]
    args = [x3]
    out_specs, out_shape = [], []
    if with_q:
        cq = wq.shape[1]
        in_specs.append(_resident(wq.shape))
        args.append(wq)
        out_specs.append(pl.BlockSpec((1, 1, tl, cq), lambda b, j: (b, 0, j, 0)))
        out_shape.append(jax.ShapeDtypeStruct((bsz, 1, l_len, cq), q_dtype))
    in_specs.append(_resident(wkvt.shape))
    args.append(wkvt)
    out_specs.append(pl.BlockSpec((1, 1, c2, tl), lambda b, j: (b, 0, 0, j)))
    out_shape.append(jax.ShapeDtypeStruct((bsz, 1, c2, l_len), kv_dtype))
    win_static, aliases = _add_window(win, lambda b, j: b * nl + j, bsz * nl, in_specs, args, out_specs, out_shape)
    return pl.pallas_call(
        functools.partial(_proj_kernel, with_q=with_q, d=1, lsub=tl, res_per_step=1, win=win_static,
                          steps_per_row=nl),
        grid=(bsz, nl),
        in_specs=in_specs, out_specs=out_specs, out_shape=out_shape,
        input_output_aliases=aliases,
        compiler_params=_cparams(2),
        name=name,
    )(*args)


def project_strided(xc, wq, wkvt, *, d, rows_per_step, name, win=None):
    bsz, _, s, _ = xc.shape
    lsub = s // d
    rps = max(1, rows_per_step // lsub)
    nj = d // rps
    c2, cq = wkvt.shape[0], wq.shape[1]
    in_specs = [pl.BlockSpec((1, N_CHUNKS, s, LANES), lambda b, j: (b, 0, 0, 0)),
                _resident(wq.shape), _resident(wkvt.shape)]
    args = [xc, wq, wkvt]
    out_specs = [pl.BlockSpec((1, rps, lsub, cq), lambda b, j: (b, j, 0, 0)),
                 pl.BlockSpec((1, rps, c2, lsub), lambda b, j: (b, j, 0, 0))]
    out_shape = [jax.ShapeDtypeStruct((bsz, d, lsub, cq), BF16),
                 jax.ShapeDtypeStruct((bsz, d, c2, lsub), BF16)]
    win_static, aliases = _add_window(win, lambda b, j: b * nj + j, bsz * nj, in_specs, args, out_specs, out_shape)
    return pl.pallas_call(
        functools.partial(_proj_kernel, with_q=True, d=d, lsub=lsub, res_per_step=rps, win=win_static,
                          steps_per_row=nj),
        grid=(bsz, nj),
        in_specs=in_specs, out_specs=out_specs, out_shape=out_shape,
        input_output_aliases=aliases,
        compiler_params=_cparams(2),
        name=name,
    )(*args)


def _bias_kernel(table_ref, idx_ref, o_ref):
    h = pl.program_id(0)
    idx = idx_ref[...]
    acc = jnp.full(idx.shape, NEG_INF, F32)
    for b in range(NUM_BUCKETS):
        acc = jnp.where(idx == b, table_ref[b * N_HEADS + h], acc)
    o_ref[0] = acc


def bias_lookup(table, idx):
    r, c = idx.shape
    return pl.pallas_call(
        _bias_kernel,
        grid_spec=pltpu.PrefetchScalarGridSpec(
            num_scalar_prefetch=1,
            grid=(N_HEADS,),
            in_specs=[pl.BlockSpec((r, c), lambda h, t: (0, 0))],
            out_specs=pl.BlockSpec((1, r, c), lambda h, t: (h, 0, 0)),
        ),
        out_shape=jax.ShapeDtypeStruct((N_HEADS, r, c), F32),
        name="bias_lookup",
    )(table.reshape(-1), idx)


def _t5_bucket(dist):
    max_exact = NUM_BUCKETS // 2
    d = jnp.maximum(dist.astype(F32), 1.0)
    large = max_exact + (jnp.log(d / max_exact) / math.log(MAX_DISTANCE / max_exact)
                         * (NUM_BUCKETS - max_exact)).astype(jnp.int32)
    large = jnp.minimum(large, NUM_BUCKETS - 1)
    return jnp.where(dist < max_exact, dist, large)


def _prompt_bias_idx(dilation, window_units):
    qi = jnp.arange(BLOCK)[:, None]
    ki = jnp.arange(2 * BLOCK)[None, :]
    rel = qi + BLOCK - ki
    band = (rel >= 0) & (rel <= window_units)
    return jnp.where(band, _t5_bucket(jnp.maximum(rel, 0) * dilation), -1).astype(jnp.int32)


def _sample_bias_idx(dilation, window_units, lb, t):
    tau = jnp.arange(t)[:, None]
    pos = jnp.arange(lb)[None, :]
    dist = lb + tau - pos
    ok = (dist % dilation == 0) & (dist // dilation <= window_units)
    idx_c = jnp.where(ok, _t5_bucket(dist), -1).astype(jnp.int32)
    nu = jnp.arange(LANES)[None, :] - (LANES - t)
    dist_n = tau - nu
    ok_n = (nu >= 0) & (dist_n >= 0) & (dist_n % dilation == 0) & (dist_n // dilation <= window_units)
    idx_n = jnp.where(ok_n, _t5_bucket(jnp.maximum(dist_n, 0)), -1).astype(jnp.int32)
    return jnp.concatenate([idx_c, idx_n], axis=1)


ATTN_ITEMS_PER_STEP = 2


def _band_attn_kernel(*refs, n_kv_heads, n_blocks, d, n_items, with_sinks):
    if with_sinks:
        sink_ref, q_ref, kvt_ref, bias_ref, o_ref, s_scr, e_scr, f_scr, fac_scr = refs
    else:
        q_ref, kvt_ref, bias_ref, o_ref, lse_ref, s_scr, e_scr, f_scr, fac_scr, l_scr = refs
    ck = n_kv_heads * HEAD_DIM
    group = N_HEADS // n_kv_heads
    by_block = n_blocks > 1
    lane_lo = lax.broadcasted_iota(jnp.int32, (BLOCK, 2 * HEAD_DIM), 1) < HEAD_DIM

    def pair_tile(a, b):
        return jnp.where(lane_lo, a, b)

    def run(it, start, nk, bias_lo):
        seq = 0 if by_block else it
        q_rows = slice(it * BLOCK, (it + 1) * BLOCK) if by_block else slice(0, BLOCK)
        for h in range(N_HEADS):
            kh = h // group
            qh = q_ref[0, seq, q_rows, h * HEAD_DIM:(h + 1) * HEAD_DIM]
            kw = kvt_ref[0, seq, kh * HEAD_DIM:(kh + 1) * HEAD_DIM, pl.ds(start, nk)]
            s_scr[h, :, :nk] = _dot(qh, kw) + bias_ref[h, :, bias_lo:bias_lo + nk]
        for hp in range(N_HEADS // 2):
            lse_ab, fac_ab = [], []
            for h in (2 * hp, 2 * hp + 1):
                s = s_scr[h, :, :nk]
                m = jnp.max(s, axis=-1, keepdims=True)
                e = jnp.exp(s - m)
                den = jnp.sum(e, axis=-1, keepdims=True)
                e_scr[h, :, :nk] = e.astype(BF16)
                lse = m + jnp.log(den)
                fac = 1.0 / den
                if with_sinks:
                    fac = fac * jax.nn.sigmoid(lse - sink_ref[h])
                lse_ab.append(lse)
                fac_ab.append(fac)
            ps = slice(hp * 2 * HEAD_DIM, (hp + 1) * 2 * HEAD_DIM)
            fac_scr[:, ps] = pair_tile(*fac_ab)
            if not with_sinks:
                l_scr[:, ps] = pair_tile(*lse_ab)
        for hp in range(N_HEADS // 2):
            kh_a, kh_b = (2 * hp) // group, (2 * hp + 1) // group
            v_a = kvt_ref[0, seq, ck + kh_a * HEAD_DIM:ck + (kh_a + 1) * HEAD_DIM, pl.ds(start, nk)]
            if kh_a == kh_b:
                vw = jnp.concatenate([v_a, v_a], axis=0)
            else:
                vw = kvt_ref[0, seq, ck + kh_a * HEAD_DIM:ck + (kh_b + 1) * HEAD_DIM, pl.ds(start, nk)]
            ps = slice(hp * 2 * HEAD_DIM, (hp + 1) * 2 * HEAD_DIM)
            acc = jnp.where(lane_lo, _dot_nt(e_scr[2 * hp, :, :nk], vw), _dot_nt(e_scr[2 * hp + 1, :, :nk], vw))
            f_scr[:, ps] = acc * fac_scr[:, ps]

    for it in range(n_items):
        if by_block:
            r = pl.program_id(1)
            i = pl.program_id(2) * n_items + it
        else:
            r = pl.program_id(1) * n_items + it
            i = 0
        if not by_block:
            run(it, 0, BLOCK, BLOCK)
        elif it > 0:
            run(it, pl.multiple_of((i - 1) * BLOCK, BLOCK), 2 * BLOCK, 0)
        else:
            @pl.when(i == 0)
            def _():
                run(it, 0, BLOCK, BLOCK)

            @pl.when(i > 0)
            def _():
                run(it, pl.multiple_of((i - 1) * BLOCK, BLOCK), 2 * BLOCK, 0)

        if d == 1:
            rows = slice(it * BLOCK, (it + 1) * BLOCK)
        else:
            rows = pl.ds(i * (BLOCK * d) + r, BLOCK, stride=d)
        _store_chunked(o_ref, f_scr[...], rows)
        if not with_sinks:
            _store_chunked(lse_ref, l_scr[...], rows)


def band_attention(q, kvt, bias, bias_block, sinks, *, name):
    bsz, d, lsub, c = q.shape
    c2 = kvt.shape[2]
    nb = lsub // BLOCK
    with_sinks = sinks is not None
    if nb > 1:
        n_items = ATTN_ITEMS_PER_STEP if nb % ATTN_ITEMS_PER_STEP == 0 else 1
        grid = (bsz, d, nb // n_items)
        q_spec = pl.BlockSpec((1, 1, n_items * BLOCK, c), lambda b, r, i, *_: (b, r, i, 0))
        kv_spec = pl.BlockSpec((1, 1, c2, lsub), lambda b, r, i, *_: (b, r, 0, 0))
    else:
        n_items = 1
        grid = (bsz, d // n_items, 1)
        q_spec = pl.BlockSpec((1, n_items, BLOCK, c), lambda b, r, i, *_: (b, r, 0, 0))
        kv_spec = pl.BlockSpec((1, n_items, c2, lsub), lambda b, r, i, *_: (b, r, 0, 0))
    kern = functools.partial(_band_attn_kernel, n_kv_heads=c2 // (2 * HEAD_DIM), n_blocks=nb, d=d,
                             n_items=n_items, with_sinks=with_sinks)
    if d == 1:
        o_spec = pl.BlockSpec((1, N_CHUNKS, n_items * BLOCK, LANES), lambda b, r, i, *_: (b, 0, i, 0))
    else:
        o_spec = pl.BlockSpec((1, N_CHUNKS, lsub * d, LANES), lambda b, r, i, *_: (b, 0, 0, 0))
    o_shape = jax.ShapeDtypeStruct((bsz, N_CHUNKS, lsub * d, LANES), F32)
    in_specs = [q_spec, kv_spec,
                pl.BlockSpec((N_HEADS, BLOCK, 2 * BLOCK), lambda b, r, i, *_: (0, 0, bias_block),
                             pipeline_mode=pl.Buffered(1))]
    scratch = [pltpu.VMEM((N_HEADS, BLOCK, 2 * BLOCK), F32), pltpu.VMEM((N_HEADS, BLOCK, 2 * BLOCK), BF16),
               pltpu.VMEM((BLOCK, c), F32), pltpu.VMEM((BLOCK, c), F32)]
    if with_sinks:
        return pl.pallas_call(
            kern,
            grid_spec=pltpu.PrefetchScalarGridSpec(
                num_scalar_prefetch=1, grid=grid, in_specs=in_specs, out_specs=o_spec, scratch_shapes=scratch),
            out_shape=o_shape, compiler_params=_cparams(3), name=name,
        )(sinks, q, kvt, bias)
    return pl.pallas_call(
        kern, grid=grid, in_specs=in_specs, out_specs=[o_spec, o_spec], out_shape=[o_shape, o_shape],
        scratch_shapes=scratch + [pltpu.VMEM((BLOCK, c), F32)],
        compiler_params=_cparams(3), name=name,
    )(q, kvt, bias)


def _window_unit(q_ref, cache_ref, new_ref, bias_ref, o_ref, lse_ref, state_ref, sink_ref, *,
                 sb, n, hc, heads_per_step, group, t, lb):
    rows = heads_per_step * HEAD_DIM
    per_tile = LANES // t
    lane = lax.broadcasted_iota(jnp.int32, (rows, LANES), 1)
    keep = lane < LANES - t
    n_tiles = lb // LANES
    shift = (LANES - t) - (n % per_tile) * t
    new_k = pltpu.roll(new_ref[0], shift, 1)
    new_v = pltpu.roll(new_ref[1], shift, 1)
    new_kb, new_vb = new_k.astype(BF16), new_v.astype(BF16)
    scores = []
    for kh in range(heads_per_step):
        rr = slice(kh * HEAD_DIM, (kh + 1) * HEAD_DIM)
        kt = jnp.concatenate([cache_ref[sb, 0, rr, :].astype(BF16), new_kb[rr]], axis=1)
        for g in range(group):
            hl = kh * group + g
            qh = q_ref[sb, :, hl * HEAD_DIM:(hl + 1) * HEAD_DIM].astype(BF16)
            scores.append(_dot(qh, kt) + bias_ref[hl])
    probs, lses = [], []
    for s in scores:
        m = jnp.max(s, axis=-1, keepdims=True)
        e = jnp.exp(s - m)
        den = jnp.sum(e, axis=-1, keepdims=True)
        probs.append((e / den).astype(BF16))
        lses.append(m + jnp.log(den))
    for kh in range(heads_per_step):
        rr = slice(kh * HEAD_DIM, (kh + 1) * HEAD_DIM)
        vt = jnp.concatenate([cache_ref[sb, 1, rr, :].astype(BF16), new_vb[rr]], axis=1)
        for g in range(group):
            hl = kh * group + g
            hs = slice(hl * HEAD_DIM, (hl + 1) * HEAD_DIM)
            out = _dot_nt(probs[hl], vt)
            if sink_ref is not None:
                h_abs = hc * (heads_per_step * group) + hl
                out = out * jax.nn.sigmoid(lses[hl] - sink_ref[h_abs])
            else:
                lse_ref[sb, :, hs] = jnp.broadcast_to(lses[hl], (t, HEAD_DIM))
            o_ref[sb, :, hs] = out
    for kv, new in ((0, new_k), (1, new_v)):
        rolled = pltpu.roll(cache_ref[sb, kv, :, 0:LANES], LANES - t, 1)
        for j in range(n_tiles):
            if j + 1 < n_tiles:
                nxt = pltpu.roll(cache_ref[sb, kv, :, (j + 1) * LANES:(j + 2) * LANES], LANES - t, 1)
            else:
                nxt = new
            state_ref[sb, kv, :, j * LANES:(j + 1) * LANES] = jnp.where(keep, rolled, nxt)
            rolled = nxt


def _window_attn_kernel(*refs, heads_per_step, group, t, lb, n_sb, with_sinks):
    if with_sinks:
        sink_ref, q_ref, cache_ref, new_ref, bias_ref, o_ref, state_ref = refs
        lse_ref = None
    else:
        q_ref, cache_ref, new_ref, bias_ref, o_ref, lse_ref, state_ref = refs
        sink_ref = None
    for sb in range(n_sb):
        _window_unit(q_ref, cache_ref, new_ref, bias_ref, o_ref, lse_ref, state_ref, sink_ref,
                     sb=sb, n=pl.program_id(0) * n_sb + sb, hc=pl.program_id(1),
                     heads_per_step=heads_per_step, group=group, t=t, lb=lb)


def window_work(q, cache_t, new_t, bias, bias_block, heads_per_step):
    n_s, t, cq = q.shape
    _, _, ck, lb = cache_t.shape
    n_kv = ck // HEAD_DIM
    return dict(q=q, cache_t=cache_t, new_t=new_t, bias=bias, bias_block=bias_block, t=t, lb=lb, n_total=n_s,
                heads_per_step=heads_per_step, group=(cq // HEAD_DIM) // n_kv, n_hc=n_kv // heads_per_step,
                n0=0, n_count=n_s, prev_state=None)


def window_specs(win, unit_of):
    t, lb, n_hc, n0 = win["t"], win["lb"], win["n_hc"], win["n0"]
    rows = win["heads_per_step"] * HEAD_DIM
    cq_step = rows * win["group"]
    per_tile = LANES // t

    def samp(*idx):
        return n0 + unit_of(*idx) // n_hc

    def chunk(*idx):
        return unit_of(*idx) % n_hc

    in_specs = [
        pl.BlockSpec((1, t, cq_step), lambda *idx: (samp(*idx), 0, chunk(*idx))),
        pl.BlockSpec((1, 2, rows, lb), lambda *idx: (samp(*idx), 0, chunk(*idx), 0)),
        pl.BlockSpec((2, rows, LANES), lambda *idx: (0, chunk(*idx), samp(*idx) // per_tile)),
        pl.BlockSpec((win["heads_per_step"] * win["group"], t, lb + LANES),
                     lambda *idx: (chunk(*idx), 0, win["bias_block"])),
    ]
    part_spec = pl.BlockSpec((1, t, cq_step), lambda *idx: (samp(*idx) - n0, 0, chunk(*idx)))
    part_shape = jax.ShapeDtypeStruct((win["n_count"], t, cq_step * n_hc), F32)
    out_specs = [part_spec, part_spec, pl.BlockSpec((1, 2, rows, lb), lambda *idx: (samp(*idx), 0, chunk(*idx), 0))]
    out_shape = [part_shape, part_shape, jax.ShapeDtypeStruct(win["cache_t"].shape, F32)]
    return in_specs, out_specs, out_shape


def window_attention(q, cache_t, new_t, bias, bias_block, sinks, *, heads_per_step, samples_per_step, name):
    n_s, t, cq = q.shape
    _, _, ck, lb = cache_t.shape
    n_kv = ck // HEAD_DIM
    group = (cq // HEAD_DIM) // n_kv
    n_hc = n_kv // heads_per_step
    rows = heads_per_step * HEAD_DIM
    cq_step = rows * group
    n_sb = samples_per_step
    steps_per_tile = (LANES // t) // n_sb
    with_sinks = sinks is not None
    kern = functools.partial(_window_attn_kernel, heads_per_step=heads_per_step, group=group, t=t, lb=lb,
                             n_sb=n_sb, with_sinks=with_sinks)
    o_spec = pl.BlockSpec((n_sb, t, cq_step), lambda n, hc, *_: (n, 0, hc))
    o_shape = jax.ShapeDtypeStruct((n_s, t, cq), F32)
    st_spec = pl.BlockSpec((n_sb, 2, rows, lb), lambda n, hc, *_: (n, 0, hc, 0))
    st_shape = jax.ShapeDtypeStruct(cache_t.shape, F32)
    in_specs = [
        pl.BlockSpec((n_sb, t, cq_step), lambda n, hc, *_: (n, 0, hc)),
        pl.BlockSpec((n_sb, 2, rows, lb), lambda n, hc, *_: (n, 0, hc, 0)),
        pl.BlockSpec((2, rows, LANES), lambda n, hc, *_: (0, hc, n // steps_per_tile)),
        pl.BlockSpec((heads_per_step * group, t, lb + LANES), lambda n, hc, *_: (hc, 0, bias_block)),
    ]
    grid = (n_s // n_sb, n_hc)
    if with_sinks:
        return pl.pallas_call(
            kern,
            grid_spec=pltpu.PrefetchScalarGridSpec(
                num_scalar_prefetch=1, grid=grid, in_specs=in_specs, out_specs=[o_spec, st_spec]),
            out_shape=[o_shape, st_shape], compiler_params=_cparams(2), name=name,
        )(sinks, q, cache_t, new_t, bias)
    return pl.pallas_call(
        kern, grid=grid, in_specs=in_specs, out_specs=[o_spec, o_spec, st_spec],
        out_shape=[o_shape, o_shape, st_shape], compiler_params=_cparams(2), name=name,
    )(q, cache_t, new_t, bias)


def _mix_out_kernel(*refs, n_groups, chunked):
    o_refs = refs[:n_groups]
    lse_refs = refs[n_groups:2 * n_groups] if n_groups > 1 else ()
    x_ref, wo_ref, g_ref, b_ref, out_ref = refs[len(o_refs) + len(lse_refs):]
    load = _load_chunked if chunked else (lambda r: r[...])
    if n_groups == 1:
        mixed = load(o_refs[0])
    else:
        lses = [load(r) for r in lse_refs]
        m = functools.reduce(jnp.maximum, lses)
        es = [jnp.exp(l - m) for l in lses]
        tot = functools.reduce(lambda a, b: a + b, es)
        mixed = functools.reduce(lambda a, b: a + b, [(e / tot) * load(r) for e, r in zip(es, o_refs)])
    y = _dot(mixed.astype(BF16), wo_ref[...])
    out_ref[...] = _layer_norm(ALPHA * x_ref[...] + y, g_ref[...], b_ref[...])


def mix_out(outs, lses, x, wo, gain, bias, tm=512):
    rows, d = x.shape
    tm = min(tm, rows)
    ng = len(outs)
    row_spec = pl.BlockSpec((tm, d), lambda i: (i, 0))
    chunked = outs[0].ndim == 4
    if chunked:
        per_seq = outs[0].shape[2] // tm
        mix_spec = pl.BlockSpec((1, N_CHUNKS, tm, LANES), lambda i: (i // per_seq, 0, i % per_seq, 0))
    else:
        mix_spec = row_spec
    return pl.pallas_call(
        functools.partial(_mix_out_kernel, n_groups=ng, chunked=chunked),
        grid=(rows // tm,),
        in_specs=[mix_spec] * (ng + len(lses)) + [row_spec, _resident(wo.shape), _resident((1, d)),
                                                   _resident((1, d))],
        out_specs=row_spec,
        out_shape=jax.ShapeDtypeStruct((rows, d), F32),
        compiler_params=_cparams(1),
        name="mix_out",
    )(*outs, *lses, x, wo, gain, bias)


WINDOW_BLOCK_BYTES = 4 * 1024 * 1024
MAX_SAMPLES_PER_STEP = 8


def _window_tiling(n_kv, lb):
    head_bytes = 2 * HEAD_DIM * lb * 4
    heads = max(1, min(n_kv, WINDOW_BLOCK_BYTES // head_bytes))
    samples = 1
    if heads == n_kv:
        samples = max(1, min(MAX_SAMPLES_PER_STEP, WINDOW_BLOCK_BYTES // (head_bytes * n_kv)))
    return heads, samples


def _pack_lane_blocks(widths):
    order = sorted(range(len(widths)), key=lambda k: -widths[k])
    blocks = [0] * len(widths)
    pos = 0
    for k in order:
        blocks[k] = -(-pos // widths[k])
        pos = (blocks[k] + 1) * widths[k]
    return blocks, pos


def _to_pos_minor(cache):
    n, lb, _, h, dh = cache.shape
    return jnp.transpose(cache, (0, 2, 3, 4, 1)).reshape(n, 2, h * dh, lb)


def _from_pos_minor(state_t, n_heads):
    n, _, _, lb = state_t.shape
    return jnp.transpose(state_t.reshape(n, 2, n_heads, HEAD_DIM, lb), (0, 4, 1, 2, 3))[None]


def kernel(x_prompt, x_sample, cache_a_kv, cache_b1_kv, cache_b2_kv, cache_b3_kv, rel_bias_table, ln_gain, ln_bias, ffn_w_gate, ffn_w_up, ffn_w_down, attn_a_w_qkv, attn_a_w_o, attn_a_sinks, attn_b_w_qkv, attn_b_w_o):
    bsz, seq, dm = x_prompt.shape
    n_s, t, _ = x_sample.shape
    hd = N_HEADS * HEAD_DIM
    ck_a = N_KV_HEADS_A * HEAD_DIM
    xp = x_prompt.reshape(bsz * seq, dm)
    xs = x_sample.reshape(n_s * t, dm)
    table = rel_bias_table.astype(F32)
    li = 0

    def ln_params(i, j):
        return ln_gain[i, j].reshape(1, dm).astype(F32), ln_bias[i, j].reshape(1, dm).astype(F32)

    wg_all, wu_all, wd_all = ffn_w_gate.astype(BF16), ffn_w_up.astype(BF16), ffn_w_down.astype(BF16)

    def half_ffn(x, i, f, j, **kw):
        g, b = ln_params(i, j)
        return ffn_ln(x, wg_all, wu_all, wd_all, (i, f), g, b, **kw)

    def split_qkv(w, q_cols, kv_cols):
        wq = (w[:, q_cols[0]:q_cols[1]] * SCALE).astype(BF16)
        wkvt = w[:, kv_cols[0]:kv_cols[1]].T.astype(BF16)
        return wq, wkvt

    prompt_keys = [(1, WINDOW_A)]
    for w, d in DILATED_GROUPS:
        if (d, w // d) not in prompt_keys:
            prompt_keys.append((d, w // d))
    bias_p = bias_lookup(table, jnp.concatenate([_prompt_bias_idx(*k) for k in prompt_keys], axis=1))
    window_keys = [(1, WINDOW_A, cache_a_kv.shape[2])]
    window_keys += [(d, w // d, c.shape[2]) for (w, d), c in zip(DILATED_GROUPS, (cache_b1_kv, cache_b2_kv, cache_b3_kv))]
    window_blocks, total = _pack_lane_blocks([lb + LANES for _, _, lb in window_keys])
    idx_w = jnp.full((t, total), -1, jnp.int32)
    for (dil, units, lb), blk in zip(window_keys, window_blocks):
        idx_w = idx_w.at[:, blk * (lb + LANES):(blk + 1) * (lb + LANES)].set(_sample_bias_idx(dil, units, lb, t))
    bias_w = bias_lookup(table, idx_w)

    def sample_work(xs, cache, wq, wkvt, n_kv, bias_block, name):
        xs3 = xs.reshape(1, n_s * t, dm)
        q, new_t = project(xs3, wq, wkvt, tl=n_s * t, l_off=0, l_len=n_s * t, kv_dtype=F32, q_dtype=F32,
                           name="proj_" + name)
        heads, samples = _window_tiling(n_kv, cache.shape[1])
        work = window_work(q.reshape(n_s, t, hd), _to_pos_minor(cache), new_t.reshape(2, n_kv * HEAD_DIM, n_s * t),
                           bias_w, bias_block, heads)
        return work, samples

    def run_alone(work, samples, sinks, name):
        return window_attention(work["q"], work["cache_t"], work["new_t"], work["bias"], work["bias_block"], sinks,
                                heads_per_step=work["heads_per_step"], samples_per_step=samples, name="win_" + name)

    wq_a, wkvt_a = split_qkv(attn_a_w_qkv[li], (0, hd), (hd, hd + 2 * ck_a))
    wo_a = attn_a_w_o[li].astype(BF16)
    sinks = attn_a_sinks[li].astype(F32)
    w_qkv_b = attn_b_w_qkv[li]
    wo_b = attn_b_w_o[li].astype(BF16)
    w_b = []
    for gi in range(len(DILATED_GROUPS)):
        base = gi * 3 * hd
        w_b.append(split_qkv(w_qkv_b, (base, base + hd), (base + hd, base + 3 * hd)))
    caches_b = (cache_b1_kv[li], cache_b2_kv[li], cache_b3_kv[li])

    xs = half_ffn(xs, 0, 0, 0)
    work, samples = sample_work(xs, cache_a_kv[li], wq_a, wkvt_a, N_KV_HEADS_A, window_blocks[0], "a_s")
    o, state_a_s = run_alone(work, samples, sinks, "a_s")
    state_a_s = _from_pos_minor(state_a_s, N_KV_HEADS_A)
    g1, b1 = ln_params(0, 1)
    xs = mix_out([o.reshape(n_s * t, hd)], [], xs, wo_a, g1, b1)
    xs = half_ffn(xs, 0, 1, 2)
    xs = half_ffn(xs, 1, 0, 0)
    n_prompt_ffn = 2 * DEPTH
    ffn_steps = (bsz * seq // min(512, bsz * seq)) * len(FFN_SPLITS)
    proj_steps = bsz * (seq // 512)
    carriers = {1: ("proj", len(DILATED_GROUPS) + 1, proj_steps), 2: ("ffn", n_prompt_ffn, ffn_steps)}
    riders = {}
    sample_res = {}
    for gi, (w, d) in enumerate(DILATED_GROUPS):
        work, samples = sample_work(xs, caches_b[gi], *w_b[gi], N_HEADS, window_blocks[gi + 1], f"b{gi}_s")
        kind, n_calls, steps = carriers.get(gi, (None, 1, 0))
        if kind is not None and samples == 1 and n_s * work["n_hc"] == n_calls * steps:
            riders[kind] = dict(gi=gi, work=work, n_calls=n_calls, parts=[])
        else:
            sample_res[gi] = run_alone(work, samples, None, f"b{gi}_s")

    def carry(kind, fn, *args, **kw):
        rider = riders.get(kind)
        if rider is None:
            return fn(*args, **kw)
        k = len(rider["parts"])
        per_call = n_s // rider["n_calls"]
        win = dict(rider["work"], n0=k * per_call, n_count=per_call,
                   prev_state=rider["parts"][-1][2] if k else None)
        *res, o, lse, st = fn(*args, win=win, **kw)
        rider["parts"].append((o, lse, st))
        return res if len(res) > 1 else res[0]

    xp = carry("ffn", half_ffn, xp, 0, 0, 0)
    keep = min(WINDOW_A, seq)
    xp3 = xp.reshape(bsz, seq, dm)
    q, kvt = project(xp3, wq_a, wkvt_a, tl=512, l_off=0, l_len=seq, kv_dtype=BF16, name="proj_a")
    (st,) = project(xp3, None, wkvt_a, tl=keep, l_off=seq - keep, l_len=keep, kv_dtype=F32, name="state_a")
    state_a_p = _from_pos_minor(st.reshape(bsz, 2, ck_a, keep), N_KV_HEADS_A)
    o = band_attention(q, kvt, bias_p, prompt_keys.index((1, WINDOW_A)), sinks, name="attn_a")
    xp = carry("ffn", half_ffn, xp, 0, 1, 2, pre=(o, wo_a, g1, b1))

    xp, xp_chunked = carry("ffn", half_ffn, xp, 1, 0, 0, chunked_seq=seq)
    g1, b1 = ln_params(1, 1)
    xp3 = xp.reshape(bsz, seq, dm)
    outs_p, lses_p, states_p = [], [], []
    for gi, (w, d) in enumerate(DILATED_GROUPS):
        wq, wkvt = w_b[gi]
        keep = min(w, seq)
        if d == 1:
            q, kvt = carry("proj", project, xp3, wq, wkvt, tl=512, l_off=0, l_len=seq, kv_dtype=BF16,
                           name=f"proj_b{gi}")
        else:
            q, kvt = carry("proj", project_strided, xp_chunked, wq, wkvt, d=d, rows_per_step=512,
                           name=f"proj_b{gi}")
        state_kw = dict(tl=min(keep, 512), l_off=seq - keep, l_len=keep, kv_dtype=F32, name=f"state_b{gi}")
        if keep == seq:
            st = carry("proj", project, xp3, None, wkvt, **state_kw)
            st = st[0] if isinstance(st, (list, tuple)) else st
        else:
            (st,) = project(xp3, None, wkvt, **state_kw)
        states_p.append(_from_pos_minor(st.reshape(bsz, 2, hd, keep), N_HEADS))
        o, lse = band_attention(q, kvt, bias_p, prompt_keys.index((d, w // d)), None, name=f"attn_b{gi}")
        outs_p.append(o)
        lses_p.append(lse)
    xp = mix_out(outs_p, lses_p, xp, wo_b, g1, b1)
    xp = carry("ffn", half_ffn, xp, 1, 1, 2)

    for rider in riders.values():
        parts = rider["parts"]
        assert len(parts) == rider["n_calls"]
        sample_res[rider["gi"]] = (jnp.concatenate([p[0] for p in parts], axis=0),
                                   jnp.concatenate([p[1] for p in parts], axis=0), parts[-1][2])
    outs_s = [sample_res[gi][0].reshape(n_s * t, hd) for gi in range(len(DILATED_GROUPS))]
    lses_s = [sample_res[gi][1].reshape(n_s * t, hd) for gi in range(len(DILATED_GROUPS))]
    states_s = [_from_pos_minor(sample_res[gi][2], N_HEADS) for gi in range(len(DILATED_GROUPS))]
    xs = mix_out(outs_s, lses_s, xs, wo_b, g1, b1)
    xs = half_ffn(xs, 1, 1, 2)

    return (xp.reshape(bsz, seq, dm), xs.reshape(n_s, t, dm),
            state_a_p, states_p[0], states_p[1], states_p[2],
            state_a_s, states_s[0], states_s[1], states_s[2])
```

```python
import functools
import math

import jax
import jax.numpy as jnp
from jax import lax
from jax.experimental import pallas as pl
from jax.experimental.pallas import tpu as pltpu

D_MODEL = 1024
HEAD_DIM = 64
N_HEADS = 16
N_KV_HEADS_A = 4
WINDOW_A = 128
DILATED_GROUPS = ((128, 1), (512, 4), (2048, 16))
BLOCK = 128
NUM_BUCKETS = 32
MAX_DISTANCE = 2048
DEPTH = 2
ALPHA = (2 * DEPTH) ** 0.25
LN_EPS = 1e-5
NEG_INF = -1e30
SCALE = HEAD_DIM ** -0.5

LANES = 128
N_CHUNKS = D_MODEL // LANES
V7X_VMEM_BYTES = 64 * 1024 * 1024
VMEM_LIMIT = 56 * 1024 * 1024
FFN_WIN_VMEM_LIMIT = 60 * 1024 * 1024

F32 = jnp.float32
BF16 = jnp.bfloat16


def _cparams(n_axes, vmem=VMEM_LIMIT):
    return pltpu.CompilerParams(dimension_semantics=("arbitrary",) * n_axes, vmem_limit_bytes=vmem)


def _resident(shape):
    nd = len(shape)
    return pl.BlockSpec(shape, lambda *_: (0,) * nd, pipeline_mode=pl.Buffered(1))


def _dot(a, b):
    return jnp.dot(a, b, preferred_element_type=F32)


def _dot_nt(a, b):
    return lax.dot_general(a, b, (((1,), (1,)), ((), ())), preferred_element_type=F32)


def _layer_norm(y, g, b):
    mu = jnp.mean(y, axis=-1, keepdims=True)
    yc = y - mu
    var = jnp.mean(yc * yc, axis=-1, keepdims=True)
    return yc * lax.rsqrt(var + LN_EPS) * g + b


FFN_SPLITS = (768, 768, 640, 640)


def _ffn_ln_kernel(*refs, splits, chunked, win, pre):
    x_ref, wg_ref, wu_ref, wd_ref, g_ref, b_ref = refs[:6]
    pos = 6
    if pre:
        mix_ref, wproj_ref, g1_ref, b1_ref = refs[pos:pos + 4]
        pos += 4
    if win is not None:
        wq_ref, wcache_ref, wnew_ref, wbias_ref = refs[pos:pos + 4]
        pos += 4 + (1 if win["aliased"] else 0)
    o_ref = refs[pos]
    pos += 1
    if chunked:
        chunked_ref = refs[pos]
        pos += 1
    if win is not None:
        wo_ref, wlse_ref, wstate_ref = refs[pos:pos + 3]
        pos += 3
    xb_scr, acc_scr = refs[pos:pos + 2]
    xin_scr = refs[pos + 2] if pre else None
    i = pl.program_id(0)
    j = pl.program_id(1)
    last = len(splits) - 1
    off = 0
    for k, size in enumerate(splits):
        sl = slice(off, off + size)
        off += size

        @pl.when(j == k)
        def _(k=k, sl=sl):
            if k == 0 and pre:
                x_in = _layer_norm(ALPHA * x_ref[...] + _dot(_load_chunked(mix_ref).astype(BF16), wproj_ref[...]),
                                   g1_ref[...], b1_ref[...])
                xin_scr[...] = x_in
                xb = x_in.astype(BF16)
                xb_scr[...] = xb
            elif k == 0:
                xb = x_ref[...].astype(BF16)
                xb_scr[...] = xb
            else:
                xb = xb_scr[...]
            gate = _dot(xb, wg_ref[:, sl])
            up = _dot(xb, wu_ref[:, sl])
            hid = gate * jax.nn.sigmoid(gate) * up
            part = _dot(hid.astype(BF16), wd_ref[sl, :])
            if k == 0:
                acc_scr[...] = part
            elif k < last:
                acc_scr[...] += part
            else:
                x_in = xin_scr[...] if pre else x_ref[...]
                y = _layer_norm(ALPHA * x_in + 0.5 * (acc_scr[...] + part), g_ref[...], b_ref[...])
                o_ref[...] = y
                if chunked:
                    _store_chunked(chunked_ref, y)
            if win is not None:
                unit = i * len(splits) + j
                _window_unit(wq_ref, wcache_ref, wnew_ref, wbias_ref, wo_ref, wlse_ref, wstate_ref, None,
                             sb=0, n=win["n0"] + unit // win["n_hc"], hc=unit % win["n_hc"],
                             heads_per_step=win["heads_per_step"], group=win["group"], t=win["t"], lb=win["lb"])


def _store_chunked(ref, y, rows=None):
    for c in range(N_CHUNKS):
        if rows is None:
            ref[0, c] = y[:, c * LANES:(c + 1) * LANES]
        else:
            ref[0, c, rows, :] = y[:, c * LANES:(c + 1) * LANES]


def _load_chunked(ref, rows=None):
    if rows is None:
        return jnp.concatenate([ref[0, c] for c in range(N_CHUNKS)], axis=1)
    return jnp.concatenate([ref[0, c, rows, :] for c in range(N_CHUNKS)], axis=1)


def ffn_ln(x, wg, wu, wd, w_index, gain, bias, tm=512, chunked_seq=None, win=None, pre=None):
    rows, d = x.shape
    d_ff = wg.shape[-1]
    tm = min(tm, rows)
    splits = FFN_SPLITS
    assert sum(splits) == d_ff
    nf = len(splits)

    def stacked(shape):
        return pl.BlockSpec((None, None) + shape, lambda i, j: w_index + (0, 0), pipeline_mode=pl.Buffered(1))

    in_specs = [
        pl.BlockSpec((tm, d), lambda i, j: (i, 0)),
        stacked((d, d_ff)), stacked((d, d_ff)), stacked((d_ff, d)),
        _resident((1, d)), _resident((1, d)),
    ]
    args = [x, wg, wu, wd, gain, bias]
    scratch = [pltpu.VMEM((tm, d), BF16), pltpu.VMEM((tm, d), F32)]
    if pre is not None:
        mix, wo, gain1, bias1 = pre
        per_mix = mix.shape[2] // tm
        in_specs += [pl.BlockSpec((1, N_CHUNKS, tm, LANES), lambda i, j: (i // per_mix, 0, i % per_mix, 0)),
                     _resident(wo.shape), _resident((1, d)), _resident((1, d))]
        args += [mix, wo, gain1, bias1]
        scratch.append(pltpu.VMEM((tm, d), F32))
    out_specs = [pl.BlockSpec((tm, d), lambda i, j: (i, 0))]
    out_shape = [jax.ShapeDtypeStruct((rows, d), F32)]
    if chunked_seq is not None:
        per_seq = chunked_seq // tm
        out_specs.append(pl.BlockSpec((1, N_CHUNKS, tm, LANES), lambda i, j: (i // per_seq, 0, i % per_seq, 0)))
        out_shape.append(jax.ShapeDtypeStruct((rows // chunked_seq, N_CHUNKS, chunked_seq, LANES), F32))
    win_static, aliases = _add_window(win, lambda i, j: i * nf + j, (rows // tm) * nf, in_specs, args, out_specs,
                                      out_shape)
    res = pl.pallas_call(
        functools.partial(_ffn_ln_kernel, splits=splits, chunked=chunked_seq is not None, win=win_static,
                          pre=pre is not None),
        grid=(rows // tm, nf),
        in_specs=in_specs,
        out_specs=out_specs,
        out_shape=out_shape,
        scratch_shapes=scratch,
        input_output_aliases=aliases,
        compiler_params=_cparams(2, FFN_WIN_VMEM_LIMIT if win is not None else VMEM_LIMIT),
        name="ffn_ln_win" if win is not None else "ffn_ln",
    )(*args)
    return res if len(res) > 1 else res[0]


def _proj_kernel(*refs, with_q, d, lsub, res_per_step, win, steps_per_row):
    refs = list(refs)
    x_ref = refs.pop(0)
    wq_ref = refs.pop(0) if with_q else None
    wkvt_ref = refs.pop(0)
    if win is not None:
        win_in = [refs.pop(0) for _ in range(4)]
        if win["aliased"]:
            refs.pop(0)
    q_ref = refs.pop(0) if with_q else None
    kvt_ref = refs.pop(0)
    if win is not None:
        unit = pl.program_id(0) * steps_per_row + pl.program_id(1)
        _window_unit(*win_in, *refs, None, sb=0, n=win["n0"] + unit // win["n_hc"], hc=unit % win["n_hc"],
                     heads_per_step=win["heads_per_step"], group=win["group"], t=win["t"], lb=win["lb"])
    if d == 1:
        xb = x_ref[0].astype(BF16)
    else:
        r0 = pl.program_id(1) * res_per_step
        xb = jnp.concatenate(
            [_load_chunked(x_ref, pl.ds(r0 + rr, lsub, stride=d)) for rr in range(res_per_step)],
            axis=0).astype(BF16)
    if with_q:
        q = _dot(xb, wq_ref[...]).astype(q_ref.dtype)
        for rr in range(res_per_step):
            q_ref[0, rr] = q[rr * lsub:(rr + 1) * lsub]
    kvt = _dot_nt(wkvt_ref[...], xb).astype(kvt_ref.dtype)
    for rr in range(res_per_step):
        kvt_ref[0, rr] = kvt[:, rr * lsub:(rr + 1) * lsub]


def _add_window(win, unit_of, n_steps, in_specs, args, out_specs, out_shape):
    if win is None:
        return None, {}
    assert win["n_count"] * win["n_hc"] == n_steps
    w_in, w_out, w_shape = window_specs(win, unit_of)
    in_specs += w_in
    args += [win["q"], win["cache_t"], win["new_t"], win["bias"]]
    aliases = {}
    if win["prev_state"] is not None:
        in_specs.append(pl.BlockSpec(memory_space=pl.ANY))
        args.append(win["prev_state"])
        aliases = {len(args) - 1: len(out_specs) + 2}
    out_specs += w_out
    out_shape += w_shape
    static = dict(n0=win["n0"], n_hc=win["n_hc"], heads_per_step=win["heads_per_step"], group=win["group"],
                  t=win["t"], lb=win["lb"], aliased=win["prev_state"] is not None)
    return static, aliases


def project(x3, wq, wkvt, *, tl, l_off, l_len, kv_dtype, name, q_dtype=BF16, win=None):
    bsz, s, dm = x3.shape
    c2 = wkvt.shape[0]
    nl = l_len // tl
    off = l_off // tl
    with_q = wq is not None
    in_specs = [pl.BlockSpec((1, tl, dm), lambda b, j: (b, j + off, 0))]
    args = [x3]
    out_specs, out_shape = [], []
    if with_q:
        cq = wq.shape[1]
        in_specs.append(_resident(wq.shape))
        args.append(wq)
        out_specs.append(pl.BlockSpec((1, 1, tl, cq), lambda b, j: (b, 0, j, 0)))
        out_shape.append(jax.ShapeDtypeStruct((bsz, 1, l_len, cq), q_dtype))
    in_specs.append(_resident(wkvt.shape))
    args.append(wkvt)
    out_specs.append(pl.BlockSpec((1, 1, c2, tl), lambda b, j: (b, 0, 0, j)))
    out_shape.append(jax.ShapeDtypeStruct((bsz, 1, c2, l_len), kv_dtype))
    win_static, aliases = _add_window(win, lambda b, j: b * nl + j, bsz * nl, in_specs, args, out_specs, out_shape)
    return pl.pallas_call(
        functools.partial(_proj_kernel, with_q=with_q, d=1, lsub=tl, res_per_step=1, win=win_static,
                          steps_per_row=nl),
        grid=(bsz, nl),
        in_specs=in_specs, out_specs=out_specs, out_shape=out_shape,
        input_output_aliases=aliases,
        compiler_params=_cparams(2),
        name=name,
    )(*args)


def project_strided(xc, wq, wkvt, *, d, rows_per_step, name, win=None):
    bsz, _, s, _ = xc.shape
    lsub = s // d
    rps = max(1, rows_per_step // lsub)
    nj = d // rps
    c2, cq = wkvt.shape[0], wq.shape[1]
    in_specs = [pl.BlockSpec((1, N_CHUNKS, s, LANES), lambda b, j: (b, 0, 0, 0)),
                _resident(wq.shape), _resident(wkvt.shape)]
    args = [xc, wq, wkvt]
    out_specs = [pl.BlockSpec((1, rps, lsub, cq), lambda b, j: (b, j, 0, 0)),
                 pl.BlockSpec((1, rps, c2, lsub), lambda b, j: (b, j, 0, 0))]
    out_shape = [jax.ShapeDtypeStruct((bsz, d, lsub, cq), BF16),
                 jax.ShapeDtypeStruct((bsz, d, c2, lsub), BF16)]
    win_static, aliases = _add_window(win, lambda b, j: b * nj + j, bsz * nj, in_specs, args, out_specs, out_shape)
    return pl.pallas_call(
        functools.partial(_proj_kernel, with_q=True, d=d, lsub=lsub, res_per_step=rps, win=win_static,
                          steps_per_row=nj),
        grid=(bsz, nj),
        in_specs=in_specs, out_specs=out_specs, out_shape=out_shape,
        input_output_aliases=aliases,
        compiler_params=_cparams(2),
        name=name,
    )(*args)


def _bias_kernel(table_ref, idx_ref, o_ref):
    h = pl.program_id(0)
    idx = idx_ref[...]
    acc = jnp.full(idx.shape, NEG_INF, F32)
    for b in range(NUM_BUCKETS):
        acc = jnp.where(idx == b, table_ref[b * N_HEADS + h], acc)
    o_ref[0] = acc


def bias_lookup(table, idx):
    r, c = idx.shape
    return pl.pallas_call(
        _bias_kernel,
        grid_spec=pltpu.PrefetchScalarGridSpec(
            num_scalar_prefetch=1,
            grid=(N_HEADS,),
            in_specs=[pl.BlockSpec((r, c), lambda h, t: (0, 0))],
            out_specs=pl.BlockSpec((1, r, c), lambda h, t: (h, 0, 0)),
        ),
        out_shape=jax.ShapeDtypeStruct((N_HEADS, r, c), F32),
        name="bias_lookup",
    )(table.reshape(-1), idx)


def _t5_bucket(dist):
    max_exact = NUM_BUCKETS // 2
    d = jnp.maximum(dist.astype(F32), 1.0)
    large = max_exact + (jnp.log(d / max_exact) / math.log(MAX_DISTANCE / max_exact)
                         * (NUM_BUCKETS - max_exact)).astype(jnp.int32)
    large = jnp.minimum(large, NUM_BUCKETS - 1)
    return jnp.where(dist < max_exact, dist, large)


def _prompt_bias_idx(dilation, window_units):
    qi = jnp.arange(BLOCK)[:, None]
    ki = jnp.arange(2 * BLOCK)[None, :]
    rel = qi + BLOCK - ki
    band = (rel >= 0) & (rel <= window_units)
    return jnp.where(band, _t5_bucket(jnp.maximum(rel, 0) * dilation), -1).astype(jnp.int32)


def _sample_bias_idx(dilation, window_units, lb, t):
    tau = jnp.arange(t)[:, None]
    pos = jnp.arange(lb)[None, :]
    dist = lb + tau - pos
    ok = (dist % dilation == 0) & (dist // dilation <= window_units)
    idx_c = jnp.where(ok, _t5_bucket(dist), -1).astype(jnp.int32)
    nu = jnp.arange(LANES)[None, :] - (LANES - t)
    dist_n = tau - nu
    ok_n = (nu >= 0) & (dist_n >= 0) & (dist_n % dilation == 0) & (dist_n // dilation <= window_units)
    idx_n = jnp.where(ok_n, _t5_bucket(jnp.maximum(dist_n, 0)), -1).astype(jnp.int32)
    return jnp.concatenate([idx_c, idx_n], axis=1)


ATTN_ITEMS_PER_STEP = 2


def _band_attn_kernel(*refs, n_kv_heads, n_blocks, d, n_items, with_sinks):
    if with_sinks:
        sink_ref, q_ref, kvt_ref, bias_ref, o_ref, s_scr, e_scr, f_scr, m_scr = refs
    else:
        q_ref, kvt_ref, bias_ref, o_ref, lse_ref, s_scr, e_scr, f_scr, m_scr, l_scr = refs
    ck = n_kv_heads * HEAD_DIM
    group = N_HEADS // n_kv_heads
    by_block = n_blocks > 1
    lane_lo = lax.broadcasted_iota(jnp.int32, (BLOCK, 2 * HEAD_DIM), 1) < HEAD_DIM

    def pair_tile(a, b):
        return jnp.where(lane_lo, a, b)

    def run(it, start, nk, bias_lo):
        seq = 0 if by_block else it
        q_rows = slice(it * BLOCK, (it + 1) * BLOCK) if by_block else slice(0, BLOCK)
        for h in range(N_HEADS):
            kh = h // group
            qh = q_ref[0, seq, q_rows, h * HEAD_DIM:(h + 1) * HEAD_DIM]
            kw = kvt_ref[0, seq, kh * HEAD_DIM:(kh + 1) * HEAD_DIM, pl.ds(start, nk)]
            s_scr[h, :, :nk] = _dot(qh, kw) + bias_ref[h, :, bias_lo:bias_lo + nk]
        for hp in range(N_HEADS // 2):
            m_ab = []
            for h in (2 * hp, 2 * hp + 1):
                s = s_scr[h, :, :nk]
                m = jnp.max(s, axis=-1, keepdims=True)
                e_scr[h, :, :nk] = jnp.exp(s - m).astype(BF16)
                m_ab.append(m)
            m_scr[:, hp * 2 * HEAD_DIM:(hp + 1) * 2 * HEAD_DIM] = pair_tile(*m_ab)
        ones = jnp.ones((HEAD_DIM, nk), BF16)
        for hp in range(N_HEADS // 2):
            kh_a, kh_b = (2 * hp) // group, (2 * hp + 1) // group
            v_a = kvt_ref[0, seq, ck + kh_a * HEAD_DIM:ck + (kh_a + 1) * HEAD_DIM, pl.ds(start, nk)]
            v_b = kvt_ref[0, seq, ck + kh_b * HEAD_DIM:ck + (kh_b + 1) * HEAD_DIM, pl.ds(start, nk)]
            t_a = _dot_nt(e_scr[2 * hp, :, :nk], jnp.concatenate([v_a, ones], axis=0))
            t_b = _dot_nt(e_scr[2 * hp + 1, :, :nk], jnp.concatenate([ones, v_b], axis=0))
            acc = jnp.where(lane_lo, t_a, t_b)
            den = pltpu.roll(jnp.where(lane_lo, t_b, t_a), HEAD_DIM, 1)
            ps = slice(hp * 2 * HEAD_DIM, (hp + 1) * 2 * HEAD_DIM)
            lse = m_scr[:, ps] + jnp.log(den)
            fac = 1.0 / den
            if with_sinks:
                fac = fac * jax.nn.sigmoid(lse - jnp.where(lane_lo, sink_ref[2 * hp], sink_ref[2 * hp + 1]))
            else:
                l_scr[:, ps] = lse
            f_scr[:, ps] = acc * fac

    for it in range(n_items):
        if by_block:
            r = pl.program_id(1)
            i = pl.program_id(2) * n_items + it
        else:
            r = pl.program_id(1) * n_items + it
            i = 0
        if not by_block:
            run(it, 0, BLOCK, BLOCK)
        elif it > 0:
            run(it, pl.multiple_of((i - 1) * BLOCK, BLOCK), 2 * BLOCK, 0)
        else:
            @pl.when(i == 0)
            def _():
                run(it, 0, BLOCK, BLOCK)

            @pl.when(i > 0)
            def _():
                run(it, pl.multiple_of((i - 1) * BLOCK, BLOCK), 2 * BLOCK, 0)

        if d == 1:
            rows = slice(it * BLOCK, (it + 1) * BLOCK)
        else:
            rows = pl.ds(i * (BLOCK * d) + r, BLOCK, stride=d)
        _store_chunked(o_ref, f_scr[...], rows)
        if not with_sinks:
            _store_chunked(lse_ref, l_scr[...], rows)


def band_attention(q, kvt, bias, bias_block, sinks, *, name):
    bsz, d, lsub, c = q.shape
    c2 = kvt.shape[2]
    nb = lsub // BLOCK
    with_sinks = sinks is not None
    if nb > 1:
        n_items = ATTN_ITEMS_PER_STEP if nb % ATTN_ITEMS_PER_STEP == 0 else 1
        grid = (bsz, d, nb // n_items)
        q_spec = pl.BlockSpec((1, 1, n_items * BLOCK, c), lambda b, r, i, *_: (b, r, i, 0))
        kv_spec = pl.BlockSpec((1, 1, c2, lsub), lambda b, r, i, *_: (b, r, 0, 0))
    else:
        n_items = 1
        grid = (bsz, d // n_items, 1)
        q_spec = pl.BlockSpec((1, n_items, BLOCK, c), lambda b, r, i, *_: (b, r, 0, 0))
        kv_spec = pl.BlockSpec((1, n_items, c2, lsub), lambda b, r, i, *_: (b, r, 0, 0))
    kern = functools.partial(_band_attn_kernel, n_kv_heads=c2 // (2 * HEAD_DIM), n_blocks=nb, d=d,
                             n_items=n_items, with_sinks=with_sinks)
    if d == 1:
        o_spec = pl.BlockSpec((1, N_CHUNKS, n_items * BLOCK, LANES), lambda b, r, i, *_: (b, 0, i, 0))
    else:
        o_spec = pl.BlockSpec((1, N_CHUNKS, lsub * d, LANES), lambda b, r, i, *_: (b, 0, 0, 0))
    o_shape = jax.ShapeDtypeStruct((bsz, N_CHUNKS, lsub * d, LANES), F32)
    in_specs = [q_spec, kv_spec,
                pl.BlockSpec((N_HEADS, BLOCK, 2 * BLOCK), lambda b, r, i, *_: (0, 0, bias_block),
                             pipeline_mode=pl.Buffered(1))]
    scratch = [pltpu.VMEM((N_HEADS, BLOCK, 2 * BLOCK), F32), pltpu.VMEM((N_HEADS, BLOCK, 2 * BLOCK), BF16),
               pltpu.VMEM((BLOCK, c), F32), pltpu.VMEM((BLOCK, c), F32)]
    if with_sinks:
        return pl.pallas_call(
            kern,
            grid_spec=pltpu.PrefetchScalarGridSpec(
                num_scalar_prefetch=1, grid=grid, in_specs=in_specs, out_specs=o_spec, scratch_shapes=scratch),
            out_shape=o_shape, compiler_params=_cparams(3), name=name,
        )(sinks, q, kvt, bias)
    return pl.pallas_call(
        kern, grid=grid, in_specs=in_specs, out_specs=[o_spec, o_spec], out_shape=[o_shape, o_shape],
        scratch_shapes=scratch + [pltpu.VMEM((BLOCK, c), F32)],
        compiler_params=_cparams(3), name=name,
    )(q, kvt, bias)


def _window_unit(q_ref, cache_ref, new_ref, bias_ref, o_ref, lse_ref, state_ref, sink_ref, *,
                 sb, n, hc, heads_per_step, group, t, lb):
    rows = heads_per_step * HEAD_DIM
    per_tile = LANES // t
    lane = lax.broadcasted_iota(jnp.int32, (rows, LANES), 1)
    keep = lane < LANES - t
    n_tiles = lb // LANES
    shift = (LANES - t) - (n % per_tile) * t
    new_k = pltpu.roll(new_ref[0], shift, 1)
    new_v = pltpu.roll(new_ref[1], shift, 1)
    new_kb, new_vb = new_k.astype(BF16), new_v.astype(BF16)
    scores = []
    for kh in range(heads_per_step):
        rr = slice(kh * HEAD_DIM, (kh + 1) * HEAD_DIM)
        kt = jnp.concatenate([cache_ref[sb, 0, rr, :].astype(BF16), new_kb[rr]], axis=1)
        for g in range(group):
            hl = kh * group + g
            qh = q_ref[sb, :, hl * HEAD_DIM:(hl + 1) * HEAD_DIM].astype(BF16)
            scores.append(_dot(qh, kt) + bias_ref[hl])
    probs, lses = [], []
    for s in scores:
        m = jnp.max(s, axis=-1, keepdims=True)
        e = jnp.exp(s - m)
        den = jnp.sum(e, axis=-1, keepdims=True)
        probs.append((e / den).astype(BF16))
        lses.append(m + jnp.log(den))
    for kh in range(heads_per_step):
        rr = slice(kh * HEAD_DIM, (kh + 1) * HEAD_DIM)
        vt = jnp.concatenate([cache_ref[sb, 1, rr, :].astype(BF16), new_vb[rr]], axis=1)
        for g in range(group):
            hl = kh * group + g
            hs = slice(hl * HEAD_DIM, (hl + 1) * HEAD_DIM)
            out = _dot_nt(probs[hl], vt)
            if sink_ref is not None:
                h_abs = hc * (heads_per_step * group) + hl
                out = out * jax.nn.sigmoid(lses[hl] - sink_ref[h_abs])
            else:
                lse_ref[sb, :, hs] = jnp.broadcast_to(lses[hl], (t, HEAD_DIM))
            o_ref[sb, :, hs] = out
    for kv, new in ((0, new_k), (1, new_v)):
        rolled = pltpu.roll(cache_ref[sb, kv, :, 0:LANES], LANES - t, 1)
        for j in range(n_tiles):
            if j + 1 < n_tiles:
                nxt = pltpu.roll(cache_ref[sb, kv, :, (j + 1) * LANES:(j + 2) * LANES], LANES - t, 1)
            else:
                nxt = new
            state_ref[sb, kv, :, j * LANES:(j + 1) * LANES] = jnp.where(keep, rolled, nxt)
            rolled = nxt


def _window_attn_kernel(*refs, heads_per_step, group, t, lb, n_sb, with_sinks):
    if with_sinks:
        sink_ref, q_ref, cache_ref, new_ref, bias_ref, o_ref, state_ref = refs
        lse_ref = None
    else:
        q_ref, cache_ref, new_ref, bias_ref, o_ref, lse_ref, state_ref = refs
        sink_ref = None
    for sb in range(n_sb):
        _window_unit(q_ref, cache_ref, new_ref, bias_ref, o_ref, lse_ref, state_ref, sink_ref,
                     sb=sb, n=pl.program_id(0) * n_sb + sb, hc=pl.program_id(1),
                     heads_per_step=heads_per_step, group=group, t=t, lb=lb)


def window_work(q, cache_t, new_t, bias, bias_block, heads_per_step):
    n_s, t, cq = q.shape
    _, _, ck, lb = cache_t.shape
    n_kv = ck // HEAD_DIM
    return dict(q=q, cache_t=cache_t, new_t=new_t, bias=bias, bias_block=bias_block, t=t, lb=lb, n_total=n_s,
                heads_per_step=heads_per_step, group=(cq // HEAD_DIM) // n_kv, n_hc=n_kv // heads_per_step,
                n0=0, n_count=n_s, prev_state=None)


def window_specs(win, unit_of):
    t, lb, n_hc, n0 = win["t"], win["lb"], win["n_hc"], win["n0"]
    rows = win["heads_per_step"] * HEAD_DIM
    cq_step = rows * win["group"]
    per_tile = LANES // t

    def samp(*idx):
        return n0 + unit_of(*idx) // n_hc

    def chunk(*idx):
        return unit_of(*idx) % n_hc

    in_specs = [
        pl.BlockSpec((1, t, cq_step), lambda *idx: (samp(*idx), 0, chunk(*idx))),
        pl.BlockSpec((1, 2, rows, lb), lambda *idx: (samp(*idx), 0, chunk(*idx), 0)),
        pl.BlockSpec((2, rows, LANES), lambda *idx: (0, chunk(*idx), samp(*idx) // per_tile)),
        pl.BlockSpec((win["heads_per_step"] * win["group"], t, lb + LANES),
                     lambda *idx: (chunk(*idx), 0, win["bias_block"])),
    ]
    part_spec = pl.BlockSpec((1, t, cq_step), lambda *idx: (samp(*idx) - n0, 0, chunk(*idx)))
    part_shape = jax.ShapeDtypeStruct((win["n_count"], t, cq_step * n_hc), F32)
    out_specs = [part_spec, part_spec, pl.BlockSpec((1, 2, rows, lb), lambda *idx: (samp(*idx), 0, chunk(*idx), 0))]
    out_shape = [part_shape, part_shape, jax.ShapeDtypeStruct(win["cache_t"].shape, F32)]
    return in_specs, out_specs, out_shape


def window_attention(q, cache_t, new_t, bias, bias_block, sinks, *, heads_per_step, samples_per_step, name):
    n_s, t, cq = q.shape
    _, _, ck, lb = cache_t.shape
    n_kv = ck // HEAD_DIM
    group = (cq // HEAD_DIM) // n_kv
    n_hc = n_kv // heads_per_step
    rows = heads_per_step * HEAD_DIM
    cq_step = rows * group
    n_sb = samples_per_step
    steps_per_tile = (LANES // t) // n_sb
    with_sinks = sinks is not None
    kern = functools.partial(_window_attn_kernel, heads_per_step=heads_per_step, group=group, t=t, lb=lb,
                             n_sb=n_sb, with_sinks=with_sinks)
    o_spec = pl.BlockSpec((n_sb, t, cq_step), lambda n, hc, *_: (n, 0, hc))
    o_shape = jax.ShapeDtypeStruct((n_s, t, cq), F32)
    st_spec = pl.BlockSpec((n_sb, 2, rows, lb), lambda n, hc, *_: (n, 0, hc, 0))
    st_shape = jax.ShapeDtypeStruct(cache_t.shape, F32)
    in_specs = [
        pl.BlockSpec((n_sb, t, cq_step), lambda n, hc, *_: (n, 0, hc)),
        pl.BlockSpec((n_sb, 2, rows, lb), lambda n, hc, *_: (n, 0, hc, 0)),
        pl.BlockSpec((2, rows, LANES), lambda n, hc, *_: (0, hc, n // steps_per_tile)),
        pl.BlockSpec((heads_per_step * group, t, lb + LANES), lambda n, hc, *_: (hc, 0, bias_block)),
    ]
    grid = (n_s // n_sb, n_hc)
    if with_sinks:
        return pl.pallas_call(
            kern,
            grid_spec=pltpu.PrefetchScalarGridSpec(
                num_scalar_prefetch=1, grid=grid, in_specs=in_specs, out_specs=[o_spec, st_spec]),
            out_shape=[o_shape, st_shape], compiler_params=_cparams(2), name=name,
        )(sinks, q, cache_t, new_t, bias)
    return pl.pallas_call(
        kern, grid=grid, in_specs=in_specs, out_specs=[o_spec, o_spec, st_spec],
        out_shape=[o_shape, o_shape, st_shape], compiler_params=_cparams(2), name=name,
    )(q, cache_t, new_t, bias)


def _mix_out_kernel(*refs, n_groups, chunked):
    o_refs = refs[:n_groups]
    lse_refs = refs[n_groups:2 * n_groups] if n_groups > 1 else ()
    x_ref, wo_ref, g_ref, b_ref, out_ref = refs[len(o_refs) + len(lse_refs):]
    load = _load_chunked if chunked else (lambda r: r[...])
    if n_groups == 1:
        mixed = load(o_refs[0])
    else:
        lses = [load(r) for r in lse_refs]
        m = functools.reduce(jnp.maximum, lses)
        es = [jnp.exp(l - m) for l in lses]
        tot = functools.reduce(lambda a, b: a + b, es)
        mixed = functools.reduce(lambda a, b: a + b, [(e / tot) * load(r) for e, r in zip(es, o_refs)])
    y = _dot(mixed.astype(BF16), wo_ref[...])
    out_ref[...] = _layer_norm(ALPHA * x_ref[...] + y, g_ref[...], b_ref[...])


def mix_out(outs, lses, x, wo, gain, bias, tm=512):
    rows, d = x.shape
    tm = min(tm, rows)
    ng = len(outs)
    row_spec = pl.BlockSpec((tm, d), lambda i: (i, 0))
    chunked = outs[0].ndim == 4
    if chunked:
        per_seq = outs[0].shape[2] // tm
        mix_spec = pl.BlockSpec((1, N_CHUNKS, tm, LANES), lambda i: (i // per_seq, 0, i % per_seq, 0))
    else:
        mix_spec = row_spec
    return pl.pallas_call(
        functools.partial(_mix_out_kernel, n_groups=ng, chunked=chunked),
        grid=(rows // tm,),
        in_specs=[mix_spec] * (ng + len(lses)) + [row_spec, _resident(wo.shape), _resident((1, d)),
                                                   _resident((1, d))],
        out_specs=row_spec,
        out_shape=jax.ShapeDtypeStruct((rows, d), F32),
        compiler_params=_cparams(1),
        name="mix_out",
    )(*outs, *lses, x, wo, gain, bias)


WINDOW_BLOCK_BYTES = 4 * 1024 * 1024
MAX_SAMPLES_PER_STEP = 8


def _window_tiling(n_kv, lb):
    head_bytes = 2 * HEAD_DIM * lb * 4
    heads = max(1, min(n_kv, WINDOW_BLOCK_BYTES // head_bytes))
    samples = 1
    if heads == n_kv:
        samples = max(1, min(MAX_SAMPLES_PER_STEP, WINDOW_BLOCK_BYTES // (head_bytes * n_kv)))
    return heads, samples


def _pack_lane_blocks(widths):
    order = sorted(range(len(widths)), key=lambda k: -widths[k])
    blocks = [0] * len(widths)
    pos = 0
    for k in order:
        blocks[k] = -(-pos // widths[k])
        pos = (blocks[k] + 1) * widths[k]
    return blocks, pos


def _to_pos_minor(cache):
    n, lb, _, h, dh = cache.shape
    return jnp.transpose(cache, (0, 2, 3, 4, 1)).reshape(n, 2, h * dh, lb)


def _from_pos_minor(state_t, n_heads):
    n, _, _, lb = state_t.shape
    return jnp.transpose(state_t.reshape(n, 2, n_heads, HEAD_DIM, lb), (0, 4, 1, 2, 3))[None]


def kernel(x_prompt, x_sample, cache_a_kv, cache_b1_kv, cache_b2_kv, cache_b3_kv, rel_bias_table, ln_gain, ln_bias, ffn_w_gate, ffn_w_up, ffn_w_down, attn_a_w_qkv, attn_a_w_o, attn_a_sinks, attn_b_w_qkv, attn_b_w_o):
    bsz, seq, dm = x_prompt.shape
    n_s, t, _ = x_sample.shape
    hd = N_HEADS * HEAD_DIM
    ck_a = N_KV_HEADS_A * HEAD_DIM
    xp = x_prompt.reshape(bsz * seq, dm)
    xs = x_sample.reshape(n_s * t, dm)
    table = rel_bias_table.astype(F32)
    li = 0

    def ln_params(i, j):
        return ln_gain[i, j].reshape(1, dm).astype(F32), ln_bias[i, j].reshape(1, dm).astype(F32)

    wg_all, wu_all, wd_all = ffn_w_gate.astype(BF16), ffn_w_up.astype(BF16), ffn_w_down.astype(BF16)

    def half_ffn(x, i, f, j, **kw):
        g, b = ln_params(i, j)
        return ffn_ln(x, wg_all, wu_all, wd_all, (i, f), g, b, **kw)

    def split_qkv(w, q_cols, kv_cols):
        wq = (w[:, q_cols[0]:q_cols[1]] * SCALE).astype(BF16)
        wkvt = w[:, kv_cols[0]:kv_cols[1]].T.astype(BF16)
        return wq, wkvt

    prompt_keys = [(1, WINDOW_A)]
    for w, d in DILATED_GROUPS:
        if (d, w // d) not in prompt_keys:
            prompt_keys.append((d, w // d))
    bias_p = bias_lookup(table, jnp.concatenate([_prompt_bias_idx(*k) for k in prompt_keys], axis=1))
    window_keys = [(1, WINDOW_A, cache_a_kv.shape[2])]
    window_keys += [(d, w // d, c.shape[2]) for (w, d), c in zip(DILATED_GROUPS, (cache_b1_kv, cache_b2_kv, cache_b3_kv))]
    window_blocks, total = _pack_lane_blocks([lb + LANES for _, _, lb in window_keys])
    idx_w = jnp.full((t, total), -1, jnp.int32)
    for (dil, units, lb), blk in zip(window_keys, window_blocks):
        idx_w = idx_w.at[:, blk * (lb + LANES):(blk + 1) * (lb + LANES)].set(_sample_bias_idx(dil, units, lb, t))
    bias_w = bias_lookup(table, idx_w)

    def sample_work(xs, cache, wq, wkvt, n_kv, bias_block, name):
        xs3 = xs.reshape(1, n_s * t, dm)
        q, new_t = project(xs3, wq, wkvt, tl=n_s * t, l_off=0, l_len=n_s * t, kv_dtype=F32, q_dtype=F32,
                           name="proj_" + name)
        heads, samples = _window_tiling(n_kv, cache.shape[1])
        work = window_work(q.reshape(n_s, t, hd), _to_pos_minor(cache), new_t.reshape(2, n_kv * HEAD_DIM, n_s * t),
                           bias_w, bias_block, heads)
        return work, samples

    def run_alone(work, samples, sinks, name):
        return window_attention(work["q"], work["cache_t"], work["new_t"], work["bias"], work["bias_block"], sinks,
                                heads_per_step=work["heads_per_step"], samples_per_step=samples, name="win_" + name)

    wq_a, wkvt_a = split_qkv(attn_a_w_qkv[li], (0, hd), (hd, hd + 2 * ck_a))
    wo_a = attn_a_w_o[li].astype(BF16)
    sinks = attn_a_sinks[li].astype(F32)
    w_qkv_b = attn_b_w_qkv[li]
    wo_b = attn_b_w_o[li].astype(BF16)
    w_b = []
    for gi in range(len(DILATED_GROUPS)):
        base = gi * 3 * hd
        w_b.append(split_qkv(w_qkv_b, (base, base + hd), (base + hd, base + 3 * hd)))
    caches_b = (cache_b1_kv[li], cache_b2_kv[li], cache_b3_kv[li])

    xs = half_ffn(xs, 0, 0, 0)
    work, samples = sample_work(xs, cache_a_kv[li], wq_a, wkvt_a, N_KV_HEADS_A, window_blocks[0], "a_s")
    o, state_a_s = run_alone(work, samples, sinks, "a_s")
    state_a_s = _from_pos_minor(state_a_s, N_KV_HEADS_A)
    g1, b1 = ln_params(0, 1)
    xs = mix_out([o.reshape(n_s * t, hd)], [], xs, wo_a, g1, b1)
    xs = half_ffn(xs, 0, 1, 2)
    xs = half_ffn(xs, 1, 0, 0)
    n_prompt_ffn = 2 * DEPTH
    ffn_steps = (bsz * seq // min(512, bsz * seq)) * len(FFN_SPLITS)
    proj_steps = bsz * (seq // 512)
    carriers = {1: ("proj", len(DILATED_GROUPS) + 1, proj_steps), 2: ("ffn", n_prompt_ffn, ffn_steps)}
    riders = {}
    sample_res = {}
    for gi, (w, d) in enumerate(DILATED_GROUPS):
        work, samples = sample_work(xs, caches_b[gi], *w_b[gi], N_HEADS, window_blocks[gi + 1], f"b{gi}_s")
        kind, n_calls, steps = carriers.get(gi, (None, 1, 0))
        if kind is not None and samples == 1 and n_s * work["n_hc"] == n_calls * steps:
            riders[kind] = dict(gi=gi, work=work, n_calls=n_calls, parts=[])
        else:
            sample_res[gi] = run_alone(work, samples, None, f"b{gi}_s")

    def carry(kind, fn, *args, **kw):
        rider = riders.get(kind)
        if rider is None:
            return fn(*args, **kw)
        k = len(rider["parts"])
        per_call = n_s // rider["n_calls"]
        win = dict(rider["work"], n0=k * per_call, n_count=per_call,
                   prev_state=rider["parts"][-1][2] if k else None)
        *res, o, lse, st = fn(*args, win=win, **kw)
        rider["parts"].append((o, lse, st))
        return res if len(res) > 1 else res[0]

    xp = carry("ffn", half_ffn, xp, 0, 0, 0)
    keep = min(WINDOW_A, seq)
    xp3 = xp.reshape(bsz, seq, dm)
    q, kvt = project(xp3, wq_a, wkvt_a, tl=512, l_off=0, l_len=seq, kv_dtype=BF16, name="proj_a")
    (st,) = project(xp3, None, wkvt_a, tl=keep, l_off=seq - keep, l_len=keep, kv_dtype=F32, name="state_a")
    state_a_p = _from_pos_minor(st.reshape(bsz, 2, ck_a, keep), N_KV_HEADS_A)
    o = band_attention(q, kvt, bias_p, prompt_keys.index((1, WINDOW_A)), sinks, name="attn_a")
    xp = carry("ffn", half_ffn, xp, 0, 1, 2, pre=(o, wo_a, g1, b1))

    xp, xp_chunked = carry("ffn", half_ffn, xp, 1, 0, 0, chunked_seq=seq)
    g1, b1 = ln_params(1, 1)
    xp3 = xp.reshape(bsz, seq, dm)
    outs_p, lses_p, states_p = [], [], []
    for gi, (w, d) in enumerate(DILATED_GROUPS):
        wq, wkvt = w_b[gi]
        keep = min(w, seq)
        if d == 1:
            q, kvt = carry("proj", project, xp3, wq, wkvt, tl=512, l_off=0, l_len=seq, kv_dtype=BF16,
                           name=f"proj_b{gi}")
        else:
            q, kvt = carry("proj", project_strided, xp_chunked, wq, wkvt, d=d, rows_per_step=512,
                           name=f"proj_b{gi}")
        state_kw = dict(tl=min(keep, 512), l_off=seq - keep, l_len=keep, kv_dtype=F32, name=f"state_b{gi}")
        if keep == seq:
            st = carry("proj", project, xp3, None, wkvt, **state_kw)
            st = st[0] if isinstance(st, (list, tuple)) else st
        else:
            (st,) = project(xp3, None, wkvt, **state_kw)
        states_p.append(_from_pos_minor(st.reshape(bsz, 2, hd, keep), N_HEADS))
        o, lse = band_attention(q, kvt, bias_p, prompt_keys.index((d, w // d)), None, name=f"attn_b{gi}")
        outs_p.append(o)
        lses_p.append(lse)
    xp = mix_out(outs_p, lses_p, xp, wo_b, g1, b1)
    xp = carry("ffn", half_ffn, xp, 1, 1, 2)

    for rider in riders.values():
        parts = rider["parts"]
        assert len(parts) == rider["n_calls"]
        sample_res[rider["gi"]] = (jnp.concatenate([p[0] for p in parts], axis=0),
                                   jnp.concatenate([p[1] for p in parts], axis=0), parts[-1][2])
    outs_s = [sample_res[gi][0].reshape(n_s * t, hd) for gi in range(len(DILATED_GROUPS))]
    lses_s = [sample_res[gi][1].reshape(n_s * t, hd) for gi in range(len(DILATED_GROUPS))]
    states_s = [_from_pos_minor(sample_res[gi][2], N_HEADS) for gi in range(len(DILATED_GROUPS))]
    xs = mix_out(outs_s, lses_s, xs, wo_b, g1, b1)
    xs = half_ffn(xs, 1, 1, 2)

    return (xp.reshape(bsz, seq, dm), xs.reshape(n_s, t, dm),
            state_a_p, states_p[0], states_p[1], states_p[2],
            state_a_s, states_s[0], states_s[1], states_s[2])
```

```python
import functools
import math

import jax
import jax.numpy as jnp
from jax import lax
from jax.experimental import pallas as pl
from jax.experimental.pallas import tpu as pltpu

D_MODEL = 1024
HEAD_DIM = 64
N_HEADS = 16
N_KV_HEADS_A = 4
WINDOW_A = 128
DILATED_GROUPS = ((128, 1), (512, 4), (2048, 16))
BLOCK = 128
NUM_BUCKETS = 32
MAX_DISTANCE = 2048
DEPTH = 2
ALPHA = (2 * DEPTH) ** 0.25
LN_EPS = 1e-5
NEG_INF = -1e30
SCALE = HEAD_DIM ** -0.5

LANES = 128
N_CHUNKS = D_MODEL // LANES
V7X_VMEM_BYTES = 64 * 1024 * 1024
VMEM_LIMIT = 56 * 1024 * 1024
FFN_WIN_VMEM_LIMIT = 60 * 1024 * 1024

F32 = jnp.float32
BF16 = jnp.bfloat16


def _cparams(n_axes, vmem=VMEM_LIMIT):
    return pltpu.CompilerParams(dimension_semantics=("arbitrary",) * n_axes, vmem_limit_bytes=vmem)


def _resident(shape):
    nd = len(shape)
    return pl.BlockSpec(shape, lambda *_: (0,) * nd, pipeline_mode=pl.Buffered(1))


def _dot(a, b):
    return jnp.dot(a, b, preferred_element_type=F32)


def _dot_nt(a, b):
    return lax.dot_general(a, b, (((1,), (1,)), ((), ())), preferred_element_type=F32)


def _layer_norm(y, g, b):
    mu = jnp.mean(y, axis=-1, keepdims=True)
    yc = y - mu
    var = jnp.mean(yc * yc, axis=-1, keepdims=True)
    return yc * lax.rsqrt(var + LN_EPS) * g + b


FFN_SPLITS = (768, 768, 640, 640)


def _ffn_ln_kernel(*refs, splits, chunked, win, pre):
    x_ref, wg_ref, wu_ref, wd_ref, g_ref, b_ref = refs[:6]
    pos = 6
    if pre:
        mix_ref, wproj_ref, g1_ref, b1_ref = refs[pos:pos + 4]
        pos += 4
    if win is not None:
        wq_ref, wcache_ref, wnew_ref, wbias_ref = refs[pos:pos + 4]
        pos += 4 + (1 if win["aliased"] else 0)
    o_ref = refs[pos]
    pos += 1
    if chunked:
        chunked_ref = refs[pos]
        pos += 1
    if win is not None:
        wo_ref, wlse_ref, wstate_ref = refs[pos:pos + 3]
        pos += 3
    xb_scr, acc_scr = refs[pos:pos + 2]
    xin_scr = refs[pos + 2] if pre else None
    i = pl.program_id(0)
    j = pl.program_id(1)
    last = len(splits) - 1
    off = 0
    for k, size in enumerate(splits):
        sl = slice(off, off + size)
        off += size

        @pl.when(j == k)
        def _(k=k, sl=sl):
            if k == 0 and pre:
                x_in = _layer_norm(ALPHA * x_ref[...] + _dot(_load_chunked(mix_ref).astype(BF16), wproj_ref[...]),
                                   g1_ref[...], b1_ref[...])
                xin_scr[...] = x_in
                xb = x_in.astype(BF16)
                xb_scr[...] = xb
            elif k == 0:
                xb = x_ref[...].astype(BF16)
                xb_scr[...] = xb
            else:
                xb = xb_scr[...]
            gate = _dot(xb, wg_ref[:, sl])
            up = _dot(xb, wu_ref[:, sl])
            hid = gate * jax.nn.sigmoid(gate) * up
            part = _dot(hid.astype(BF16), wd_ref[sl, :])
            if k == 0:
                acc_scr[...] = part
            elif k < last:
                acc_scr[...] += part
            else:
                x_in = xin_scr[...] if pre else x_ref[...]
                y = _layer_norm(ALPHA * x_in + 0.5 * (acc_scr[...] + part), g_ref[...], b_ref[...])
                o_ref[...] = y
                if chunked:
                    _store_chunked(chunked_ref, y)
            if win is not None:
                unit = i * len(splits) + j
                _window_unit(wq_ref, wcache_ref, wnew_ref, wbias_ref, wo_ref, wlse_ref, wstate_ref, None,
                             sb=0, n=win["n0"] + unit // win["n_hc"], hc=unit % win["n_hc"],
                             heads_per_step=win["heads_per_step"], group=win["group"], t=win["t"], lb=win["lb"])


def _store_chunked(ref, y, rows=None):
    for c in range(N_CHUNKS):
        if rows is None:
            ref[0, c] = y[:, c * LANES:(c + 1) * LANES]
        else:
            ref[0, c, rows, :] = y[:, c * LANES:(c + 1) * LANES]


def _load_chunked(ref, rows=None):
    if rows is None:
        return jnp.concatenate([ref[0, c] for c in range(N_CHUNKS)], axis=1)
    return jnp.concatenate([ref[0, c, rows, :] for c in range(N_CHUNKS)], axis=1)


def ffn_ln(x, wg, wu, wd, w_index, gain, bias, tm=512, chunked_seq=None, win=None, pre=None):
    rows, d = x.shape
    d_ff = wg.shape[-1]
    tm = min(tm, rows)
    splits = FFN_SPLITS
    assert sum(splits) == d_ff
    nf = len(splits)

    def stacked(shape):
        return pl.BlockSpec((None, None) + shape, lambda i, j: w_index + (0, 0), pipeline_mode=pl.Buffered(1))

    in_specs = [
        pl.BlockSpec((tm, d), lambda i, j: (i, 0)),
        stacked((d, d_ff)), stacked((d, d_ff)), stacked((d_ff, d)),
        _resident((1, d)), _resident((1, d)),
    ]
    args = [x, wg, wu, wd, gain, bias]
    scratch = [pltpu.VMEM((tm, d), BF16), pltpu.VMEM((tm, d), F32)]
    if pre is not None:
        mix, wo, gain1, bias1 = pre
        per_mix = mix.shape[2] // tm
        in_specs += [pl.BlockSpec((1, N_CHUNKS, tm, LANES), lambda i, j: (i // per_mix, 0, i % per_mix, 0)),
                     _resident(wo.shape), _resident((1, d)), _resident((1, d))]
        args += [mix, wo, gain1, bias1]
        scratch.append(pltpu.VMEM((tm, d), F32))
    out_specs = [pl.BlockSpec((tm, d), lambda i, j: (i, 0))]
    out_shape = [jax.ShapeDtypeStruct((rows, d), F32)]
    if chunked_seq is not None:
        per_seq = chunked_seq // tm
        out_specs.append(pl.BlockSpec((1, N_CHUNKS, tm, LANES), lambda i, j: (i // per_seq, 0, i % per_seq, 0)))
        out_shape.append(jax.ShapeDtypeStruct((rows // chunked_seq, N_CHUNKS, chunked_seq, LANES), F32))
    win_static, aliases = _add_window(win, lambda i, j: i * nf + j, (rows // tm) * nf, in_specs, args, out_specs,
                                      out_shape)
    res = pl.pallas_call(
        functools.partial(_ffn_ln_kernel, splits=splits, chunked=chunked_seq is not None, win=win_static,
                          pre=pre is not None),
        grid=(rows // tm, nf),
        in_specs=in_specs,
        out_specs=out_specs,
        out_shape=out_shape,
        scratch_shapes=scratch,
        input_output_aliases=aliases,
        compiler_params=_cparams(2, FFN_WIN_VMEM_LIMIT if win is not None else VMEM_LIMIT),
        name="ffn_ln_win" if win is not None else "ffn_ln",
    )(*args)
    return res if len(res) > 1 else res[0]


def _proj_kernel(*refs, with_q, d, lsub, res_per_step, win, steps_per_row):
    refs = list(refs)
    x_ref = refs.pop(0)
    wq_ref = refs.pop(0) if with_q else None
    wkvt_ref = refs.pop(0)
    if win is not None:
        win_in = [refs.pop(0) for _ in range(4)]
        if win["aliased"]:
            refs.pop(0)
    q_ref = refs.pop(0) if with_q else None
    kvt_ref = refs.pop(0)
    if win is not None:
        unit = pl.program_id(0) * steps_per_row + pl.program_id(1)
        _window_unit(*win_in, *refs, None, sb=0, n=win["n0"] + unit // win["n_hc"], hc=unit % win["n_hc"],
                     heads_per_step=win["heads_per_step"], group=win["group"], t=win["t"], lb=win["lb"])
    if d == 1:
        xb = x_ref[0].astype(BF16)
    else:
        r0 = pl.program_id(1) * res_per_step
        xb = jnp.concatenate(
            [_load_chunked(x_ref, pl.ds(r0 + rr, lsub, stride=d)) for rr in range(res_per_step)],
            axis=0).astype(BF16)
    if with_q:
        q = _dot(xb, wq_ref[...]).astype(q_ref.dtype)
        for rr in range(res_per_step):
            q_ref[0, rr] = q[rr * lsub:(rr + 1) * lsub]
    kvt = _dot_nt(wkvt_ref[...], xb).astype(kvt_ref.dtype)
    for rr in range(res_per_step):
        kvt_ref[0, rr] = kvt[:, rr * lsub:(rr + 1) * lsub]


def _add_window(win, unit_of, n_steps, in_specs, args, out_specs, out_shape):
    if win is None:
        return None, {}
    assert win["n_count"] * win["n_hc"] == n_steps
    w_in, w_out, w_shape = window_specs(win, unit_of)
    in_specs += w_in
    args += [win["q"], win["cache_t"], win["new_t"], win["bias"]]
    aliases = {}
    if win["prev_state"] is not None:
        in_specs.append(pl.BlockSpec(memory_space=pl.ANY))
        args.append(win["prev_state"])
        aliases = {len(args) - 1: len(out_specs) + 2}
    out_specs += w_out
    out_shape += w_shape
    static = dict(n0=win["n0"], n_hc=win["n_hc"], heads_per_step=win["heads_per_step"], group=win["group"],
                  t=win["t"], lb=win["lb"], aliased=win["prev_state"] is not None)
    return static, aliases


def project(x3, wq, wkvt, *, tl, l_off, l_len, kv_dtype, name, q_dtype=BF16, win=None):
    bsz, s, dm = x3.shape
    c2 = wkvt.shape[0]
    nl = l_len // tl
    off = l_off // tl
    with_q = wq is not None
    in_specs = [pl.BlockSpec((1, tl, dm), lambda b, j: (b, j + off, 0))]
    args = [x3]
    out_specs, out_shape = [], []
    if with_q:
        cq = wq.shape[1]
        in_specs.append(_resident(wq.shape))
        args.append(wq)
        out_specs.append(pl.BlockSpec((1, 1, tl, cq), lambda b, j: (b, 0, j, 0)))
        out_shape.append(jax.ShapeDtypeStruct((bsz, 1, l_len, cq), q_dtype))
    in_specs.append(_resident(wkvt.shape))
    args.append(wkvt)
    out_specs.append(pl.BlockSpec((1, 1, c2, tl), lambda b, j: (b, 0, 0, j)))
    out_shape.append(jax.ShapeDtypeStruct((bsz, 1, c2, l_len), kv_dtype))
    win_static, aliases = _add_window(win, lambda b, j: b * nl + j, bsz * nl, in_specs, args, out_specs, out_shape)
    return pl.pallas_call(
        functools.partial(_proj_kernel, with_q=with_q, d=1, lsub=tl, res_per_step=1, win=win_static,
                          steps_per_row=nl),
        grid=(bsz, nl),
        in_specs=in_specs, out_specs=out_specs, out_shape=out_shape,
        input_output_aliases=aliases,
        compiler_params=_cparams(2),
        name=name,
    )(*args)


def project_strided(xc, wq, wkvt, *, d, rows_per_step, name, win=None):
    bsz, _, s, _ = xc.shape
    lsub = s // d
    rps = max(1, rows_per_step // lsub)
    nj = d // rps
    c2, cq = wkvt.shape[0], wq.shape[1]
    in_specs = [pl.BlockSpec((1, N_CHUNKS, s, LANES), lambda b, j: (b, 0, 0, 0)),
                _resident(wq.shape), _resident(wkvt.shape)]
    args = [xc, wq, wkvt]
    out_specs = [pl.BlockSpec((1, rps, lsub, cq), lambda b, j: (b, j, 0, 0)),
                 pl.BlockSpec((1, rps, c2, lsub), lambda b, j: (b, j, 0, 0))]
    out_shape = [jax.ShapeDtypeStruct((bsz, d, lsub, cq), BF16),
                 jax.ShapeDtypeStruct((bsz, d, c2, lsub), BF16)]
    win_static, aliases = _add_window(win, lambda b, j: b * nj + j, bsz * nj, in_specs, args, out_specs, out_shape)
    return pl.pallas_call(
        functools.partial(_proj_kernel, with_q=True, d=d, lsub=lsub, res_per_step=rps, win=win_static,
                          steps_per_row=nj),
        grid=(bsz, nj),
        in_specs=in_specs, out_specs=out_specs, out_shape=out_shape,
        input_output_aliases=aliases,
        compiler_params=_cparams(2),
        name=name,
    )(*args)


def _bias_kernel(table_ref, idx_ref, o_ref):
    h = pl.program_id(0)
    idx = idx_ref[...]
    acc = jnp.full(idx.shape, NEG_INF, F32)
    for b in range(NUM_BUCKETS):
        acc = jnp.where(idx == b, table_ref[b * N_HEADS + h], acc)
    o_ref[0] = acc


def bias_lookup(table, idx):
    r, c = idx.shape
    return pl.pallas_call(
        _bias_kernel,
        grid_spec=pltpu.PrefetchScalarGridSpec(
            num_scalar_prefetch=1,
            grid=(N_HEADS,),
            in_specs=[pl.BlockSpec((r, c), lambda h, t: (0, 0))],
            out_specs=pl.BlockSpec((1, r, c), lambda h, t: (h, 0, 0)),
        ),
        out_shape=jax.ShapeDtypeStruct((N_HEADS, r, c), F32),
        name="bias_lookup",
    )(table.reshape(-1), idx)


def _t5_bucket(dist):
    max_exact = NUM_BUCKETS // 2
    d = jnp.maximum(dist.astype(F32), 1.0)
    large = max_exact + (jnp.log(d / max_exact) / math.log(MAX_DISTANCE / max_exact)
                         * (NUM_BUCKETS - max_exact)).astype(jnp.int32)
    large = jnp.minimum(large, NUM_BUCKETS - 1)
    return jnp.where(dist < max_exact, dist, large)


def _prompt_bias_idx(dilation, window_units):
    qi = jnp.arange(BLOCK)[:, None]
    ki = jnp.arange(2 * BLOCK)[None, :]
    rel = qi + BLOCK - ki
    band = (rel >= 0) & (rel <= window_units)
    return jnp.where(band, _t5_bucket(jnp.maximum(rel, 0) * dilation), -1).astype(jnp.int32)


def _sample_bias_idx(dilation, window_units, lb, t):
    tau = jnp.arange(t)[:, None]
    pos = jnp.arange(lb)[None, :]
    dist = lb + tau - pos
    ok = (dist % dilation == 0) & (dist // dilation <= window_units)
    idx_c = jnp.where(ok, _t5_bucket(dist), -1).astype(jnp.int32)
    nu = jnp.arange(LANES)[None, :] - (LANES - t)
    dist_n = tau - nu
    ok_n = (nu >= 0) & (dist_n >= 0) & (dist_n % dilation == 0) & (dist_n // dilation <= window_units)
    idx_n = jnp.where(ok_n, _t5_bucket(jnp.maximum(dist_n, 0)), -1).astype(jnp.int32)
    return jnp.concatenate([idx_c, idx_n], axis=1)


ATTN_ITEMS_PER_STEP = 2


def _band_attn_kernel(*refs, n_kv_heads, n_blocks, d, n_items, with_sinks):
    if with_sinks:
        sink_ref, q_ref, kvt_ref, bias_ref, o_ref, s_scr, e_scr, f_scr, m_scr = refs
    else:
        q_ref, kvt_ref, bias_ref, o_ref, lse_ref, s_scr, e_scr, f_scr, m_scr, l_scr = refs
    ck = n_kv_heads * HEAD_DIM
    group = N_HEADS // n_kv_heads
    by_block = n_blocks > 1
    lane_lo = lax.broadcasted_iota(jnp.int32, (BLOCK, 2 * HEAD_DIM), 1) < HEAD_DIM

    def pair_tile(a, b):
        return jnp.where(lane_lo, a, b)

    def run(it, start, nk, bias_lo):
        seq = 0 if by_block else it
        q_rows = slice(it * BLOCK, (it + 1) * BLOCK) if by_block else slice(0, BLOCK)
        for h in range(N_HEADS):
            kh = h // group
            qh = q_ref[0, seq, q_rows, h * HEAD_DIM:(h + 1) * HEAD_DIM]
            kw = kvt_ref[0, seq, kh * HEAD_DIM:(kh + 1) * HEAD_DIM, pl.ds(start, nk)]
            s_scr[h, :, :nk] = _dot(qh, kw) + bias_ref[h, :, bias_lo:bias_lo + nk]
        for hp in range(N_HEADS // 2):
            m_ab = []
            for h in (2 * hp, 2 * hp + 1):
                s = s_scr[h, :, :nk]
                m = jnp.max(s, axis=-1, keepdims=True)
                e_scr[h, :, :nk] = jnp.exp(s - m).astype(BF16)
                m_ab.append(m)
            m_scr[:, hp * 2 * HEAD_DIM:(hp + 1) * 2 * HEAD_DIM] = pair_tile(*m_ab)
        ones = jnp.ones((HEAD_DIM, nk), BF16)
        for hp in range(N_HEADS // 2):
            kh_a, kh_b = (2 * hp) // group, (2 * hp + 1) // group
            v_a = kvt_ref[0, seq, ck + kh_a * HEAD_DIM:ck + (kh_a + 1) * HEAD_DIM, pl.ds(start, nk)]
            v_b = kvt_ref[0, seq, ck + kh_b * HEAD_DIM:ck + (kh_b + 1) * HEAD_DIM, pl.ds(start, nk)]
            t_a = _dot_nt(e_scr[2 * hp, :, :nk], jnp.concatenate([v_a, ones], axis=0))
            t_b = _dot_nt(e_scr[2 * hp + 1, :, :nk], jnp.concatenate([ones, v_b], axis=0))
            acc = jnp.where(lane_lo, t_a, t_b)
            den = pltpu.roll(jnp.where(lane_lo, t_b, t_a), HEAD_DIM, 1)
            ps = slice(hp * 2 * HEAD_DIM, (hp + 1) * 2 * HEAD_DIM)
            lse = m_scr[:, ps] + jnp.log(den)
            fac = 1.0 / den
            if with_sinks:
                fac = fac * jax.nn.sigmoid(lse - jnp.where(lane_lo, sink_ref[2 * hp], sink_ref[2 * hp + 1]))
            else:
                l_scr[:, ps] = lse
            f_scr[:, ps] = acc * fac

    for it in range(n_items):
        if by_block:
            r = pl.program_id(1)
            i = pl.program_id(2) * n_items + it
        else:
            r = pl.program_id(1) * n_items + it
            i = 0
        if not by_block:
            run(it, 0, BLOCK, BLOCK)
        elif it > 0:
            run(it, pl.multiple_of((i - 1) * BLOCK, BLOCK), 2 * BLOCK, 0)
        else:
            @pl.when(i == 0)
            def _():
                run(it, 0, BLOCK, BLOCK)

            @pl.when(i > 0)
            def _():
                run(it, pl.multiple_of((i - 1) * BLOCK, BLOCK), 2 * BLOCK, 0)

        if d == 1:
            rows = slice(it * BLOCK, (it + 1) * BLOCK)
        else:
            rows = pl.ds(i * (BLOCK * d) + r, BLOCK, stride=d)
        _store_chunked(o_ref, f_scr[...], rows)
        if not with_sinks:
            _store_chunked(lse_ref, l_scr[...], rows)


def band_attention(q, kvt, bias, bias_block, sinks, *, name):
    bsz, d, lsub, c = q.shape
    c2 = kvt.shape[2]
    nb = lsub // BLOCK
    with_sinks = sinks is not None
    if nb > 1:
        n_items = ATTN_ITEMS_PER_STEP if nb % ATTN_ITEMS_PER_STEP == 0 else 1
        grid = (bsz, d, nb // n_items)
        q_spec = pl.BlockSpec((1, 1, n_items * BLOCK, c), lambda b, r, i, *_: (b, r, i, 0))
        kv_spec = pl.BlockSpec((1, 1, c2, lsub), lambda b, r, i, *_: (b, r, 0, 0))
    else:
        n_items = 1
        grid = (bsz, d // n_items, 1)
        q_spec = pl.BlockSpec((1, n_items, BLOCK, c), lambda b, r, i, *_: (b, r, 0, 0))
        kv_spec = pl.BlockSpec((1, n_items, c2, lsub), lambda b, r, i, *_: (b, r, 0, 0))
    kern = functools.partial(_band_attn_kernel, n_kv_heads=c2 // (2 * HEAD_DIM), n_blocks=nb, d=d,
                             n_items=n_items, with_sinks=with_sinks)
    if d == 1:
        o_spec = pl.BlockSpec((1, N_CHUNKS, n_items * BLOCK, LANES), lambda b, r, i, *_: (b, 0, i, 0))
    else:
        o_spec = pl.BlockSpec((1, N_CHUNKS, lsub * d, LANES), lambda b, r, i, *_: (b, 0, 0, 0))
    o_shape = jax.ShapeDtypeStruct((bsz, N_CHUNKS, lsub * d, LANES), F32)
    in_specs = [q_spec, kv_spec,
                pl.BlockSpec((N_HEADS, BLOCK, 2 * BLOCK), lambda b, r, i, *_: (0, 0, bias_block),
                             pipeline_mode=pl.Buffered(1))]
    scratch = [pltpu.VMEM((N_HEADS, BLOCK, 2 * BLOCK), F32), pltpu.VMEM((N_HEADS, BLOCK, 2 * BLOCK), BF16),
               pltpu.VMEM((BLOCK, c), F32), pltpu.VMEM((BLOCK, c), F32)]
    if with_sinks:
        return pl.pallas_call(
            kern,
            grid_spec=pltpu.PrefetchScalarGridSpec(
                num_scalar_prefetch=1, grid=grid, in_specs=in_specs, out_specs=o_spec, scratch_shapes=scratch),
            out_shape=o_shape, compiler_params=_cparams(3), name=name,
        )(sinks, q, kvt, bias)
    return pl.pallas_call(
        kern, grid=grid, in_specs=in_specs, out_specs=[o_spec, o_spec], out_shape=[o_shape, o_shape],
        scratch_shapes=scratch + [pltpu.VMEM((BLOCK, c), F32)],
        compiler_params=_cparams(3), name=name,
    )(q, kvt, bias)


def _window_unit(q_ref, cache_ref, new_ref, bias_ref, o_ref, lse_ref, state_ref, sink_ref, *,
                 sb, n, hc, heads_per_step, group, t, lb):
    sbs = sb if isinstance(sb, tuple) else (sb,)
    ns = n if isinstance(n, tuple) else (n,)
    rows = heads_per_step * HEAD_DIM
    per_tile = LANES // t
    lane = lax.broadcasted_iota(jnp.int32, (rows, LANES), 1)
    keep = lane < LANES - t
    n_tiles = lb // LANES
    new_k, new_v, new_kb, new_vb = {}, {}, {}, {}
    for s_, n_ in zip(sbs, ns):
        shift = (LANES - t) - (n_ % per_tile) * t
        new_k[s_] = pltpu.roll(new_ref[0], shift, 1)
        new_v[s_] = pltpu.roll(new_ref[1], shift, 1)
        new_kb[s_], new_vb[s_] = new_k[s_].astype(BF16), new_v[s_].astype(BF16)
    scores = {}
    for s_ in sbs:
        for kh in range(heads_per_step):
            rr = slice(kh * HEAD_DIM, (kh + 1) * HEAD_DIM)
            kt = jnp.concatenate([cache_ref[s_, 0, rr, :].astype(BF16), new_kb[s_][rr]], axis=1)
            for g in range(group):
                hl = kh * group + g
                qh = q_ref[s_, :, hl * HEAD_DIM:(hl + 1) * HEAD_DIM].astype(BF16)
                scores[s_, hl] = _dot(qh, kt) + bias_ref[hl]
    probs, lses = {}, {}
    for key, s in scores.items():
        m = jnp.max(s, axis=-1, keepdims=True)
        e = jnp.exp(s - m)
        den = jnp.sum(e, axis=-1, keepdims=True)
        probs[key] = (e / den).astype(BF16)
        lses[key] = m + jnp.log(den)
    for s_ in sbs:
        for kh in range(heads_per_step):
            rr = slice(kh * HEAD_DIM, (kh + 1) * HEAD_DIM)
            vt = jnp.concatenate([cache_ref[s_, 1, rr, :].astype(BF16), new_vb[s_][rr]], axis=1)
            for g in range(group):
                hl = kh * group + g
                hs = slice(hl * HEAD_DIM, (hl + 1) * HEAD_DIM)
                out = _dot_nt(probs[s_, hl], vt)
                if sink_ref is not None:
                    h_abs = hc * (heads_per_step * group) + hl
                    out = out * jax.nn.sigmoid(lses[s_, hl] - sink_ref[h_abs])
                else:
                    lse_ref[s_, :, hs] = jnp.broadcast_to(lses[s_, hl], (t, HEAD_DIM))
                o_ref[s_, :, hs] = out
    for s_ in sbs:
        for kv, new in ((0, new_k[s_]), (1, new_v[s_])):
            rolled = pltpu.roll(cache_ref[s_, kv, :, 0:LANES], LANES - t, 1)
            for j in range(n_tiles):
                if j + 1 < n_tiles:
                    nxt = pltpu.roll(cache_ref[s_, kv, :, (j + 1) * LANES:(j + 2) * LANES], LANES - t, 1)
                else:
                    nxt = new
                state_ref[s_, kv, :, j * LANES:(j + 1) * LANES] = jnp.where(keep, rolled, nxt)
                rolled = nxt


def _window_attn_kernel(*refs, heads_per_step, group, t, lb, n_sb, with_sinks):
    if with_sinks:
        sink_ref, q_ref, cache_ref, new_ref, bias_ref, o_ref, state_ref = refs
        lse_ref = None
    else:
        q_ref, cache_ref, new_ref, bias_ref, o_ref, lse_ref, state_ref = refs
        sink_ref = None
    _window_unit(q_ref, cache_ref, new_ref, bias_ref, o_ref, lse_ref, state_ref, sink_ref,
                 sb=tuple(range(n_sb)), n=tuple(pl.program_id(0) * n_sb + sb for sb in range(n_sb)),
                 hc=pl.program_id(1), heads_per_step=heads_per_step, group=group, t=t, lb=lb)


def window_work(q, cache_t, new_t, bias, bias_block, heads_per_step):
    n_s, t, cq = q.shape
    _, _, ck, lb = cache_t.shape
    n_kv = ck // HEAD_DIM
    return dict(q=q, cache_t=cache_t, new_t=new_t, bias=bias, bias_block=bias_block, t=t, lb=lb, n_total=n_s,
                heads_per_step=heads_per_step, group=(cq // HEAD_DIM) // n_kv, n_hc=n_kv // heads_per_step,
                n0=0, n_count=n_s, prev_state=None)


def window_specs(win, unit_of):
    t, lb, n_hc, n0 = win["t"], win["lb"], win["n_hc"], win["n0"]
    rows = win["heads_per_step"] * HEAD_DIM
    cq_step = rows * win["group"]
    per_tile = LANES // t

    def samp(*idx):
        return n0 + unit_of(*idx) // n_hc

    def chunk(*idx):
        return unit_of(*idx) % n_hc

    in_specs = [
        pl.BlockSpec((1, t, cq_step), lambda *idx: (samp(*idx), 0, chunk(*idx))),
        pl.BlockSpec((1, 2, rows, lb), lambda *idx: (samp(*idx), 0, chunk(*idx), 0)),
        pl.BlockSpec((2, rows, LANES), lambda *idx: (0, chunk(*idx), samp(*idx) // per_tile)),
        pl.BlockSpec((win["heads_per_step"] * win["group"], t, lb + LANES),
                     lambda *idx: (chunk(*idx), 0, win["bias_block"])),
    ]
    part_spec = pl.BlockSpec((1, t, cq_step), lambda *idx: (samp(*idx) - n0, 0, chunk(*idx)))
    part_shape = jax.ShapeDtypeStruct((win["n_count"], t, cq_step * n_hc), F32)
    out_specs = [part_spec, part_spec, pl.BlockSpec((1, 2, rows, lb), lambda *idx: (samp(*idx), 0, chunk(*idx), 0))]
    out_shape = [part_shape, part_shape, jax.ShapeDtypeStruct(win["cache_t"].shape, F32)]
    return in_specs, out_specs, out_shape


def window_attention(q, cache_t, new_t, bias, bias_block, sinks, *, heads_per_step, samples_per_step, name):
    n_s, t, cq = q.shape
    _, _, ck, lb = cache_t.shape
    n_kv = ck // HEAD_DIM
    group = (cq // HEAD_DIM) // n_kv
    n_hc = n_kv // heads_per_step
    rows = heads_per_step * HEAD_DIM
    cq_step = rows * group
    n_sb = samples_per_step
    steps_per_tile = (LANES // t) // n_sb
    with_sinks = sinks is not None
    kern = functools.partial(_window_attn_kernel, heads_per_step=heads_per_step, group=group, t=t, lb=lb,
                             n_sb=n_sb, with_sinks=with_sinks)
    o_spec = pl.BlockSpec((n_sb, t, cq_step), lambda n, hc, *_: (n, 0, hc))
    o_shape = jax.ShapeDtypeStruct((n_s, t, cq), F32)
    st_spec = pl.BlockSpec((n_sb, 2, rows, lb), lambda n, hc, *_: (n, 0, hc, 0))
    st_shape = jax.ShapeDtypeStruct(cache_t.shape, F32)
    in_specs = [
        pl.BlockSpec((n_sb, t, cq_step), lambda n, hc, *_: (n, 0, hc)),
        pl.BlockSpec((n_sb, 2, rows, lb), lambda n, hc, *_: (n, 0, hc, 0)),
        pl.BlockSpec((2, rows, LANES), lambda n, hc, *_: (0, hc, n // steps_per_tile)),
        pl.BlockSpec((heads_per_step * group, t, lb + LANES), lambda n, hc, *_: (hc, 0, bias_block)),
    ]
    grid = (n_s // n_sb, n_hc)
    if with_sinks:
        return pl.pallas_call(
            kern,
            grid_spec=pltpu.PrefetchScalarGridSpec(
                num_scalar_prefetch=1, grid=grid, in_specs=in_specs, out_specs=[o_spec, st_spec]),
            out_shape=[o_shape, st_shape], compiler_params=_cparams(2), name=name,
        )(sinks, q, cache_t, new_t, bias)
    return pl.pallas_call(
        kern, grid=grid, in_specs=in_specs, out_specs=[o_spec, o_spec, st_spec],
        out_shape=[o_shape, o_shape, st_shape], compiler_params=_cparams(2), name=name,
    )(q, cache_t, new_t, bias)


def _mix_out_kernel(*refs, n_groups, chunked):
    o_refs = refs[:n_groups]
    lse_refs = refs[n_groups:2 * n_groups] if n_groups > 1 else ()
    x_ref, wo_ref, g_ref, b_ref, out_ref = refs[len(o_refs) + len(lse_refs):]
    load = _load_chunked if chunked else (lambda r: r[...])
    if n_groups == 1:
        mixed = load(o_refs[0])
    else:
        lses = [load(r) for r in lse_refs]
        m = functools.reduce(jnp.maximum, lses)
        es = [jnp.exp(l - m) for l in lses]
        tot = functools.reduce(lambda a, b: a + b, es)
        mixed = functools.reduce(lambda a, b: a + b, [(e / tot) * load(r) for e, r in zip(es, o_refs)])
    y = _dot(mixed.astype(BF16), wo_ref[...])
    out_ref[...] = _layer_norm(ALPHA * x_ref[...] + y, g_ref[...], b_ref[...])


def mix_out(outs, lses, x, wo, gain, bias, tm=512):
    rows, d = x.shape
    tm = min(tm, rows)
    ng = len(outs)
    row_spec = pl.BlockSpec((tm, d), lambda i: (i, 0))
    chunked = outs[0].ndim == 4
    if chunked:
        per_seq = outs[0].shape[2] // tm
        mix_spec = pl.BlockSpec((1, N_CHUNKS, tm, LANES), lambda i: (i // per_seq, 0, i % per_seq, 0))
    else:
        mix_spec = row_spec
    return pl.pallas_call(
        functools.partial(_mix_out_kernel, n_groups=ng, chunked=chunked),
        grid=(rows // tm,),
        in_specs=[mix_spec] * (ng + len(lses)) + [row_spec, _resident(wo.shape), _resident((1, d)),
                                                   _resident((1, d))],
        out_specs=row_spec,
        out_shape=jax.ShapeDtypeStruct((rows, d), F32),
        compiler_params=_cparams(1),
        name="mix_out",
    )(*outs, *lses, x, wo, gain, bias)


WINDOW_BLOCK_BYTES = 4 * 1024 * 1024
MAX_SAMPLES_PER_STEP = 8


def _window_tiling(n_kv, lb):
    head_bytes = 2 * HEAD_DIM * lb * 4
    heads = max(1, min(n_kv, WINDOW_BLOCK_BYTES // head_bytes))
    samples = 1
    if heads == n_kv:
        samples = max(1, min(MAX_SAMPLES_PER_STEP, WINDOW_BLOCK_BYTES // (head_bytes * n_kv)))
    return heads, samples


def _pack_lane_blocks(widths):
    order = sorted(range(len(widths)), key=lambda k: -widths[k])
    blocks = [0] * len(widths)
    pos = 0
    for k in order:
        blocks[k] = -(-pos // widths[k])
        pos = (blocks[k] + 1) * widths[k]
    return blocks, pos


def _to_pos_minor(cache):
    n, lb, _, h, dh = cache.shape
    return jnp.transpose(cache, (0, 2, 3, 4, 1)).reshape(n, 2, h * dh, lb)


def _from_pos_minor(state_t, n_heads):
    n, _, _, lb = state_t.shape
    return jnp.transpose(state_t.reshape(n, 2, n_heads, HEAD_DIM, lb), (0, 4, 1, 2, 3))[None]


def kernel(x_prompt, x_sample, cache_a_kv, cache_b1_kv, cache_b2_kv, cache_b3_kv, rel_bias_table, ln_gain, ln_bias, ffn_w_gate, ffn_w_up, ffn_w_down, attn_a_w_qkv, attn_a_w_o, attn_a_sinks, attn_b_w_qkv, attn_b_w_o):
    bsz, seq, dm = x_prompt.shape
    n_s, t, _ = x_sample.shape
    hd = N_HEADS * HEAD_DIM
    ck_a = N_KV_HEADS_A * HEAD_DIM
    xp = x_prompt.reshape(bsz * seq, dm)
    xs = x_sample.reshape(n_s * t, dm)
    table = rel_bias_table.astype(F32)
    li = 0

    def ln_params(i, j):
        return ln_gain[i, j].reshape(1, dm).astype(F32), ln_bias[i, j].reshape(1, dm).astype(F32)

    wg_all, wu_all, wd_all = ffn_w_gate.astype(BF16), ffn_w_up.astype(BF16), ffn_w_down.astype(BF16)

    def half_ffn(x, i, f, j, **kw):
        g, b = ln_params(i, j)
        return ffn_ln(x, wg_all, wu_all, wd_all, (i, f), g, b, **kw)

    def split_qkv(w, q_cols, kv_cols):
        wq = (w[:, q_cols[0]:q_cols[1]] * SCALE).astype(BF16)
        wkvt = w[:, kv_cols[0]:kv_cols[1]].T.astype(BF16)
        return wq, wkvt

    prompt_keys = [(1, WINDOW_A)]
    for w, d in DILATED_GROUPS:
        if (d, w // d) not in prompt_keys:
            prompt_keys.append((d, w // d))
    bias_p = bias_lookup(table, jnp.concatenate([_prompt_bias_idx(*k) for k in prompt_keys], axis=1))
    window_keys = [(1, WINDOW_A, cache_a_kv.shape[2])]
    window_keys += [(d, w // d, c.shape[2]) for (w, d), c in zip(DILATED_GROUPS, (cache_b1_kv, cache_b2_kv, cache_b3_kv))]
    window_blocks, total = _pack_lane_blocks([lb + LANES for _, _, lb in window_keys])
    idx_w = jnp.full((t, total), -1, jnp.int32)
    for (dil, units, lb), blk in zip(window_keys, window_blocks):
        idx_w = idx_w.at[:, blk * (lb + LANES):(blk + 1) * (lb + LANES)].set(_sample_bias_idx(dil, units, lb, t))
    bias_w = bias_lookup(table, idx_w)

    def sample_work(xs, cache, wq, wkvt, n_kv, bias_block, name):
        xs3 = xs.reshape(1, n_s * t, dm)
        q, new_t = project(xs3, wq, wkvt, tl=n_s * t, l_off=0, l_len=n_s * t, kv_dtype=F32, q_dtype=F32,
                           name="proj_" + name)
        heads, samples = _window_tiling(n_kv, cache.shape[1])
        work = window_work(q.reshape(n_s, t, hd), _to_pos_minor(cache), new_t.reshape(2, n_kv * HEAD_DIM, n_s * t),
                           bias_w, bias_block, heads)
        return work, samples

    def run_alone(work, samples, sinks, name):
        return window_attention(work["q"], work["cache_t"], work["new_t"], work["bias"], work["bias_block"], sinks,
                                heads_per_step=work["heads_per_step"], samples_per_step=samples, name="win_" + name)

    wq_a, wkvt_a = split_qkv(attn_a_w_qkv[li], (0, hd), (hd, hd + 2 * ck_a))
    wo_a = attn_a_w_o[li].astype(BF16)
    sinks = attn_a_sinks[li].astype(F32)
    w_qkv_b = attn_b_w_qkv[li]
    wo_b = attn_b_w_o[li].astype(BF16)
    w_b = []
    for gi in range(len(DILATED_GROUPS)):
        base = gi * 3 * hd
        w_b.append(split_qkv(w_qkv_b, (base, base + hd), (base + hd, base + 3 * hd)))
    caches_b = (cache_b1_kv[li], cache_b2_kv[li], cache_b3_kv[li])

    xs = half_ffn(xs, 0, 0, 0)
    work, samples = sample_work(xs, cache_a_kv[li], wq_a, wkvt_a, N_KV_HEADS_A, window_blocks[0], "a_s")
    o, state_a_s = run_alone(work, samples, sinks, "a_s")
    state_a_s = _from_pos_minor(state_a_s, N_KV_HEADS_A)
    g1, b1 = ln_params(0, 1)
    xs = mix_out([o.reshape(n_s * t, hd)], [], xs, wo_a, g1, b1)
    xs = half_ffn(xs, 0, 1, 2)
    xs = half_ffn(xs, 1, 0, 0)
    n_prompt_ffn = 2 * DEPTH
    ffn_steps = (bsz * seq // min(512, bsz * seq)) * len(FFN_SPLITS)
    proj_steps = bsz * (seq // 512)
    carriers = {1: ("proj", len(DILATED_GROUPS) + 1, proj_steps), 2: ("ffn", n_prompt_ffn, ffn_steps)}
    riders = {}
    sample_res = {}
    for gi, (w, d) in enumerate(DILATED_GROUPS):
        work, samples = sample_work(xs, caches_b[gi], *w_b[gi], N_HEADS, window_blocks[gi + 1], f"b{gi}_s")
        kind, n_calls, steps = carriers.get(gi, (None, 1, 0))
        if kind is not None and samples == 1 and n_s * work["n_hc"] == n_calls * steps:
            riders[kind] = dict(gi=gi, work=work, n_calls=n_calls, parts=[])
        else:
            sample_res[gi] = run_alone(work, samples, None, f"b{gi}_s")

    def carry(kind, fn, *args, **kw):
        rider = riders.get(kind)
        if rider is None:
            return fn(*args, **kw)
        k = len(rider["parts"])
        per_call = n_s // rider["n_calls"]
        win = dict(rider["work"], n0=k * per_call, n_count=per_call,
                   prev_state=rider["parts"][-1][2] if k else None)
        *res, o, lse, st = fn(*args, win=win, **kw)
        rider["parts"].append((o, lse, st))
        return res if len(res) > 1 else res[0]

    xp = carry("ffn", half_ffn, xp, 0, 0, 0)
    keep = min(WINDOW_A, seq)
    xp3 = xp.reshape(bsz, seq, dm)
    q, kvt = project(xp3, wq_a, wkvt_a, tl=512, l_off=0, l_len=seq, kv_dtype=BF16, name="proj_a")
    (st,) = project(xp3, None, wkvt_a, tl=keep, l_off=seq - keep, l_len=keep, kv_dtype=F32, name="state_a")
    state_a_p = _from_pos_minor(st.reshape(bsz, 2, ck_a, keep), N_KV_HEADS_A)
    o = band_attention(q, kvt, bias_p, prompt_keys.index((1, WINDOW_A)), sinks, name="attn_a")
    xp = carry("ffn", half_ffn, xp, 0, 1, 2, pre=(o, wo_a, g1, b1))

    xp, xp_chunked = carry("ffn", half_ffn, xp, 1, 0, 0, chunked_seq=seq)
    g1, b1 = ln_params(1, 1)
    xp3 = xp.reshape(bsz, seq, dm)
    outs_p, lses_p, states_p = [], [], []
    for gi, (w, d) in enumerate(DILATED_GROUPS):
        wq, wkvt = w_b[gi]
        keep = min(w, seq)
        if d == 1:
            q, kvt = carry("proj", project, xp3, wq, wkvt, tl=512, l_off=0, l_len=seq, kv_dtype=BF16,
                           name=f"proj_b{gi}")
        else:
            q, kvt = carry("proj", project_strided, xp_chunked, wq, wkvt, d=d, rows_per_step=512,
                           name=f"proj_b{gi}")
        state_kw = dict(tl=min(keep, 512), l_off=seq - keep, l_len=keep, kv_dtype=F32, name=f"state_b{gi}")
        if keep == seq:
            st = carry("proj", project, xp3, None, wkvt, **state_kw)
            st = st[0] if isinstance(st, (list, tuple)) else st
        else:
            (st,) = project(xp3, None, wkvt, **state_kw)
        states_p.append(_from_pos_minor(st.reshape(bsz, 2, hd, keep), N_HEADS))
        o, lse = band_attention(q, kvt, bias_p, prompt_keys.index((d, w // d)), None, name=f"attn_b{gi}")
        outs_p.append(o)
        lses_p.append(lse)
    xp = mix_out(outs_p, lses_p, xp, wo_b, g1, b1)
    xp = carry("ffn", half_ffn, xp, 1, 1, 2)

    for rider in riders.values():
        parts = rider["parts"]
        assert len(parts) == rider["n_calls"]
        sample_res[rider["gi"]] = (jnp.concatenate([p[0] for p in parts], axis=0),
                                   jnp.concatenate([p[1] for p in parts], axis=0), parts[-1][2])
    outs_s = [sample_res[gi][0].reshape(n_s * t, hd) for gi in range(len(DILATED_GROUPS))]
    lses_s = [sample_res[gi][1].reshape(n_s * t, hd) for gi in range(len(DILATED_GROUPS))]
    states_s = [_from_pos_minor(sample_res[gi][2], N_HEADS) for gi in range(len(DILATED_GROUPS))]
    xs = mix_out(outs_s, lses_s, xs, wo_b, g1, b1)
    xs = half_ffn(xs, 1, 1, 2)

    return (xp.reshape(bsz, seq, dm), xs.reshape(n_s, t, dm),
            state_a_p, states_p[0], states_p[1], states_p[2],
            state_a_s, states_s[0], states_s[1], states_s[2])
```
